```python
import jax
import jax.numpy as jnp
from jax import lax
import numpy as np

D_MODEL = 1024
BATCH = 8
SEQ = 2048
DEPTH = 2

D_FF = 2816
DN_HEADS = 4
DN_HEAD_DIM = 128
DN_KEY_DIM = DN_HEADS * DN_HEAD_DIM
DN_VAL_DIM = DN_HEADS * DN_HEAD_DIM
DN_CONV = 4
DN_CHUNK = 64
CV_CHANNELS = 512
CV_KERNEL = 31
MLA_HEADS = 4
MLA_Q_LORA = 384
MLA_KV_LORA = 256
MLA_NOPE = 128
MLA_ROPE = 64
MLA_V = 128
ATTN_BLOCK = 128
ROPE_THETA = 10000.0
N_BRANCHES = 3
DEEPNORM_ALPHA = (2 * DEPTH) ** 0.25
DEEPNORM_BETA = (8 * DEPTH) ** -0.25
NORM_EPS = 1e-5
IN_SIZES = (DN_KEY_DIM, DN_KEY_DIM, DN_VAL_DIM, DN_VAL_DIM, DN_HEADS, DN_HEADS,
            2 * CV_CHANNELS, MLA_Q_LORA, MLA_KV_LORA, MLA_ROPE, N_BRANCHES * D_MODEL)
D_IN = sum(IN_SIZES)

kernel_name = 'hybrid_deltanet_conformer_mla_block'


def _split_points():
    return [int(v) for v in np.cumsum(IN_SIZES)[:-1]]


def layer_norm(x, g, b):
    xf = x.astype(jnp.float32)
    mu = jnp.mean(xf, axis=-1, keepdims=True)
    var = jnp.mean(jnp.square(xf - mu), axis=-1, keepdims=True)
    return ((xf - mu) * lax.rsqrt(var + NORM_EPS) * g + b).astype(x.dtype)


def rms_norm(x, w):
    xf = x.astype(jnp.float32)
    return (xf * lax.rsqrt(jnp.mean(jnp.square(xf), axis=-1, keepdims=True) + NORM_EPS) * w).astype(x.dtype)


def l2_normalize(x):
    xf = x.astype(jnp.float32)
    return xf * lax.rsqrt(jnp.sum(jnp.square(xf), axis=-1, keepdims=True) + 1e-6)


def swiglu(x, w_gate, w_up, w_down):
    return (jax.nn.silu(x @ w_gate) * (x @ w_up)) @ w_down


def causal_depthwise_conv(x, w):
    k, c = w.shape
    return lax.conv_general_dilated(
        x, w[:, None, :].astype(x.dtype), window_strides=(1,), padding=[(k - 1, 0)],
        dimension_numbers=('NWC', 'WIO', 'NWC'), feature_group_count=c)


def rope_tables(positions):
    half = MLA_ROPE // 2
    inv_freq = ROPE_THETA ** (-jnp.arange(half, dtype=jnp.float32) / half)
    ang = positions.astype(jnp.float32)[..., None] * inv_freq
    return jnp.cos(ang), jnp.sin(ang)


def apply_rope(x, cos, sin):
    half = x.shape[-1] // 2
    xf = x.astype(jnp.float32)
    x1, x2 = xf[..., :half], xf[..., half:]
    return jnp.concatenate([x1 * cos - x2 * sin, x2 * cos + x1 * sin], axis=-1).astype(x.dtype)


def gated_delta_rule(q, k, v, g, beta):
    b, t, h, dk = q.shape
    dv = v.shape[-1]
    c = DN_CHUNK
    n = t // c
    f32 = jnp.float32

    def chunks(a):
        a = a.reshape(b, n, c, h, *a.shape[3:])
        return jnp.moveaxis(a, (1, 3), (0, 2))

    q = chunks(q.astype(f32)) * (dk ** -0.5)
    k = chunks(k.astype(f32))
    v = chunks(v.astype(f32))
    beta = chunks(beta.astype(f32))
    g = jnp.cumsum(chunks(g.astype(f32)), axis=-1)
    causal = jnp.tril(jnp.ones((c, c), dtype=bool))
    strict = jnp.tril(jnp.ones((c, c), dtype=bool), -1)
    decay = jnp.exp(jnp.where(causal, g[..., :, None] - g[..., None, :], -jnp.inf))
    kk = jnp.einsum('nbhik,nbhjk->nbhij', k, k)
    a_mat = jnp.where(strict, beta[..., None] * kk * decay, 0.0) + jnp.eye(c, dtype=f32)
    rhs = jnp.concatenate([v * beta[..., None], k * (beta * jnp.exp(g))[..., None]], axis=-1)
    sol = lax.linalg.triangular_solve(a_mat, rhs, left_side=True, lower=True, unit_diagonal=True)
    u, w = sol[..., :dv], sol[..., dv:]
    qk = jnp.where(causal, jnp.einsum('nbhik,nbhjk->nbhij', q, k) * decay, 0.0)
    q_dec = q * jnp.exp(g)[..., None]
    g_last = g[..., -1]
    k_dec = k * jnp.exp(g_last[..., None] - g)[..., None]

    def step(state, xs):
        u_i, w_i, qk_i, q_dec_i, k_dec_i, gl_i = xs
        v_new = u_i - jnp.einsum('bhck,bhkv->bhcv', w_i, state)
        o = jnp.einsum('bhck,bhkv->bhcv', q_dec_i, state) + jnp.einsum('bhij,bhjv->bhiv', qk_i, v_new)
        state = state * jnp.exp(gl_i)[..., None, None] + jnp.einsum('bhck,bhcv->bhkv', k_dec_i, v_new)
        return state, o

    s0 = jnp.zeros((b, h, dk, dv), f32)
    _, o = lax.scan(step, s0, (u, w, qk, q_dec, k_dec, g_last))
    return jnp.moveaxis(o, (0, 2), (1, 3)).reshape(b, t, h, dv)


def deltanet_branch(q, k, v, z, a, beta_logit, conv_w, a_log, dt_bias, norm_w, w_o):
    b, t, _ = q.shape
    qkv = jax.nn.silu(causal_depthwise_conv(jnp.concatenate([q, k, v], axis=-1), conv_w))
    q, k, v = jnp.split(qkv, [DN_KEY_DIM, 2 * DN_KEY_DIM], axis=-1)
    q = l2_normalize(q.reshape(b, t, DN_HEADS, DN_HEAD_DIM))
    k = l2_normalize(k.reshape(b, t, DN_HEADS, DN_HEAD_DIM))
    v = v.reshape(b, t, DN_HEADS, DN_HEAD_DIM)
    beta = jax.nn.sigmoid(beta_logit.astype(jnp.float32))
    g = -jnp.exp(a_log.astype(jnp.float32)) * jax.nn.softplus(a.astype(jnp.float32) + dt_bias.astype(jnp.float32))
    o = gated_delta_rule(q, k, v, g, beta)
    o = rms_norm(o, norm_w) * jax.nn.silu(z.reshape(b, t, DN_HEADS, DN_HEAD_DIM).astype(jnp.float32))
    return o.reshape(b, t, DN_VAL_DIM).astype(z.dtype) @ w_o


def conv_branch(u, glu_b, dw_w, dw_b, ln_g, ln_b, w_pw2, b_pw2):
    u = u + glu_b
    h = u[..., :CV_CHANNELS] * jax.nn.sigmoid(u[..., CV_CHANNELS:])
    h = causal_depthwise_conv(h, dw_w) + dw_b
    h = jax.nn.silu(layer_norm(h, ln_g, ln_b))
    return h @ w_pw2 + b_pw2


def causal_block_attention(q_nope, q_rope, k_nope, k_rope, v):
    t = q_nope.shape[1]
    scale = (MLA_NOPE + MLA_ROPE) ** -0.5
    outs = []
    for i in range(t // ATTN_BLOCK):
        q0, q1 = i * ATTN_BLOCK, (i + 1) * ATTN_BLOCK
        s = (jnp.einsum('bqhd,bkhd->bhqk', q_nope[:, q0:q1], k_nope[:, :q1])
             + jnp.einsum('bqhr,bkr->bhqk', q_rope[:, q0:q1], k_rope[:, :q1])).astype(jnp.float32) * scale
        mask = jnp.arange(q1)[None, :] <= jnp.arange(q0, q1)[:, None]
        p = jax.nn.softmax(jnp.where(mask, s, -jnp.inf), axis=-1).astype(v.dtype)
        outs.append(jnp.einsum('bhqk,bkhv->bqhv', p, v[:, :q1]))
    return jnp.concatenate(outs, axis=1)


def mla_branch(c_q, c_kv, k_r, cos, sin, q_norm_w, w_uq, kv_norm_w, w_ukv, w_o):
    b, t, _ = c_q.shape
    q = (rms_norm(c_q, q_norm_w) @ w_uq).reshape(b, t, MLA_HEADS, MLA_NOPE + MLA_ROPE)
    q_nope = q[..., :MLA_NOPE]
    q_rope = apply_rope(q[..., MLA_NOPE:], cos[:, :, None, :], sin[:, :, None, :])
    kv = (rms_norm(c_kv, kv_norm_w) @ w_ukv).reshape(b, t, MLA_HEADS, MLA_NOPE + MLA_V)
    k_nope, v = kv[..., :MLA_NOPE], kv[..., MLA_NOPE:]
    k_rope = apply_rope(k_r, cos, sin)
    o = causal_block_attention(q_nope, q_rope, k_nope, k_rope, v)
    return o.reshape(b, t, MLA_HEADS * MLA_V) @ w_o


def hybrid_mixer(x, cos, sin, w_in, b_gate,
                 dn_conv_w, dn_a_log, dn_dt_bias, dn_norm_w, dn_w_o,
                 cv_glu_b, cv_dw_w, cv_dw_b, cv_ln_g, cv_ln_b, cv_w_pw2, cv_b_pw2,
                 mla_q_norm_w, mla_w_uq, mla_kv_norm_w, mla_w_ukv, mla_w_o, w_out):
    b, t, d = x.shape
    proj = x @ w_in
    (dq, dk, dv, dz, da, dbeta, glu, c_q, c_kv, k_r, gate_logits) = jnp.split(proj, _split_points(), axis=-1)
    y_dn = deltanet_branch(dq, dk, dv, dz, da, dbeta, dn_conv_w, dn_a_log, dn_dt_bias, dn_norm_w, dn_w_o)
    y_cv = conv_branch(glu, cv_glu_b, cv_dw_w, cv_dw_b, cv_ln_g, cv_ln_b, cv_w_pw2, cv_b_pw2)
    y_mla = mla_branch(c_q, c_kv, k_r, cos, sin, mla_q_norm_w, mla_w_uq, mla_kv_norm_w, mla_w_ukv, mla_w_o)
    gates = jax.nn.sigmoid((gate_logits + b_gate).astype(jnp.float32)).astype(x.dtype).reshape(b, t, N_BRANCHES, d)
    merged = gates[:, :, 0] * y_dn + gates[:, :, 1] * y_cv + gates[:, :, 2] * y_mla
    return merged @ w_out


def setup_inputs(seed: int = 0) -> dict:
    key = jax.random.key(seed)
    ks = iter(jax.random.split(key, 48))
    L = DEPTH
    f32 = jnp.float32

    def w(shape, fan_in, scale=1.0):
        return jax.random.normal(next(ks), shape, f32) * (scale * fan_in ** -0.5)

    def gain(shape):
        return 1.0 + 0.02 * jax.random.normal(next(ks), shape, f32)

    def bias(shape):
        return 0.02 * jax.random.normal(next(ks), shape, f32)

    x = jax.random.normal(next(ks), (BATCH, SEQ, D_MODEL), f32)
    positions = (jax.random.randint(next(ks), (BATCH, 1), 0, 4096, dtype=jnp.int32)
                 + jnp.arange(SEQ, dtype=jnp.int32)[None, :])
    dt = jnp.exp(jax.random.uniform(next(ks), (L, DN_HEADS), f32, np.log(1e-3), np.log(1e-1)))
    return {
        'x': x,
        'positions': positions,
        'ln1_g': gain((L, D_MODEL)),
        'ln1_b': bias((L, D_MODEL)),
        'ffn1_w_gate': w((L, D_MODEL, D_FF), D_MODEL),
        'ffn1_w_up': w((L, D_MODEL, D_FF), D_MODEL),
        'ffn1_w_down': w((L, D_FF, D_MODEL), D_FF, DEEPNORM_BETA),
        'w_in': w((L, D_MODEL, D_IN), D_MODEL),
        'b_gate': bias((L, N_BRANCHES * D_MODEL)),
        'dn_conv_w': w((L, DN_CONV, 2 * DN_KEY_DIM + DN_VAL_DIM), DN_CONV),
        'dn_a_log': jnp.log(jax.random.uniform(next(ks), (L, DN_HEADS), f32, 1.0, 16.0)),
        'dn_dt_bias': dt + jnp.log(-jnp.expm1(-dt)),
        'dn_norm_w': gain((L, DN_HEAD_DIM)),
        'dn_w_o': w((L, DN_VAL_DIM, D_MODEL), DN_VAL_DIM, DEEPNORM_BETA),
        'cv_glu_b': bias((L, 2 * CV_CHANNELS)),
        'cv_dw_w': w((L, CV_KERNEL, CV_CHANNELS), CV_KERNEL),
        'cv_dw_b': bias((L, CV_CHANNELS)),
        'cv_ln_g': gain((L, CV_CHANNELS)),
        'cv_ln_b': bias((L, CV_CHANNELS)),
        'cv_w_pw2': w((L, CV_CHANNELS, D_MODEL), CV_CHANNELS, DEEPNORM_BETA),
        'cv_b_pw2': bias((L, D_MODEL)),
        'mla_q_norm_w': gain((L, MLA_Q_LORA)),
        'mla_w_uq': w((L, MLA_Q_LORA, MLA_HEADS * (MLA_NOPE + MLA_ROPE)), MLA_Q_LORA),
        'mla_kv_norm_w': gain((L, MLA_KV_LORA)),
        'mla_w_ukv': w((L, MLA_KV_LORA, MLA_HEADS * (MLA_NOPE + MLA_V)), MLA_KV_LORA),
        'mla_w_o': w((L, MLA_HEADS * MLA_V, D_MODEL), MLA_HEADS * MLA_V, DEEPNORM_BETA),
        'w_out': w((L, D_MODEL, D_MODEL), D_MODEL, DEEPNORM_BETA),
        'ln2_g': gain((L, D_MODEL)),
        'ln2_b': bias((L, D_MODEL)),
        'ffn2_w_gate': w((L, D_MODEL, D_FF), D_MODEL),
        'ffn2_w_up': w((L, D_MODEL, D_FF), D_MODEL),
        'ffn2_w_down': w((L, D_FF, D_MODEL), D_FF, DEEPNORM_BETA),
        'ln3_g': gain((L, D_MODEL)),
        'ln3_b': bias((L, D_MODEL)),
    }


def reference(x, positions, ln1_g, ln1_b, ffn1_w_gate, ffn1_w_up, ffn1_w_down, w_in, b_gate,
              dn_conv_w, dn_a_log, dn_dt_bias, dn_norm_w, dn_w_o,
              cv_glu_b, cv_dw_w, cv_dw_b, cv_ln_g, cv_ln_b, cv_w_pw2, cv_b_pw2,
              mla_q_norm_w, mla_w_uq, mla_kv_norm_w, mla_w_ukv, mla_w_o,
              w_out, ln2_g, ln2_b, ffn2_w_gate, ffn2_w_up, ffn2_w_down, ln3_g, ln3_b):
    cos, sin = rope_tables(positions)
    h = x
    for l in range(DEPTH):
        h = layer_norm(DEEPNORM_ALPHA * h + 0.5 * swiglu(h, ffn1_w_gate[l], ffn1_w_up[l], ffn1_w_down[l]),
                       ln1_g[l], ln1_b[l])
        mix = hybrid_mixer(h, cos, sin, w_in[l], b_gate[l],
                           dn_conv_w[l], dn_a_log[l], dn_dt_bias[l], dn_norm_w[l], dn_w_o[l],
                           cv_glu_b[l], cv_dw_w[l], cv_dw_b[l], cv_ln_g[l], cv_ln_b[l], cv_w_pw2[l], cv_b_pw2[l],
                           mla_q_norm_w[l], mla_w_uq[l], mla_kv_norm_w[l], mla_w_ukv[l], mla_w_o[l], w_out[l])
        h = layer_norm(DEEPNORM_ALPHA * h + mix, ln2_g[l], ln2_b[l])
        h = layer_norm(DEEPNORM_ALPHA * h + 0.5 * swiglu(h, ffn2_w_gate[l], ffn2_w_up[l], ffn2_w_down[l]),
                       ln3_g[l], ln3_b[l])
    return h
```

```python
import functools
import math

import jax
import jax.numpy as jnp
import numpy as np
from jax import lax
from jax.experimental import pallas as pl
from jax.experimental.pallas import tpu as pltpu

D_MODEL = 1024
DEPTH = 2
D_FF = 2816
DN_HEADS = 4
DN_HEAD_DIM = 128
DN_DIM = DN_HEADS * DN_HEAD_DIM
DN_CONV = 4
DN_CHUNK = 64
CV_CHANNELS = 512
CV_KERNEL = 31
MLA_HEADS = 4
MLA_Q_LORA = 384
MLA_KV_LORA = 256
MLA_NOPE = 128
MLA_ROPE = 64
MLA_V = 128
ROPE_THETA = 10000.0
N_BRANCHES = 3
DEEPNORM_ALPHA = (2 * DEPTH) ** 0.25
NORM_EPS = 1e-5

LANES = 128
SUBLANES = 8
MLA_QK_PAD = 256
VMEM_LIMIT = 56 * 1024 * 1024

F32 = jnp.float32
BF16 = jnp.bfloat16
HIGHEST = lax.Precision.HIGHEST


def _dot(a, b, precision=None):
    return jnp.dot(a, b, preferred_element_type=F32, precision=precision)


def _dot_nt(a, b, precision=None):
    return lax.dot_general(a, b, (((1,), (1,)), ((), ())), preferred_element_type=F32, precision=precision)


def _dot_tn(a, b):
    return lax.dot_general(a, b, (((0,), (0,)), ((), ())), preferred_element_type=F32)


def _sigmoid(x):
    return 1.0 / (1.0 + jnp.exp(-x))


def _silu(x):
    return x * _sigmoid(x)


def _layer_norm(y, g, b):
    mu = jnp.mean(y, axis=-1, keepdims=True)
    d = y - mu
    var = jnp.mean(d * d, axis=-1, keepdims=True)
    return d * lax.rsqrt(var + NORM_EPS) * g + b


def _resident(shape):
    nd = len(shape)
    return pl.BlockSpec(shape, lambda *_: (0,) * nd, pipeline_mode=pl.Buffered(1))


def _params(sem):
    return pltpu.CompilerParams(dimension_semantics=sem, vmem_limit_bytes=VMEM_LIMIT)


FFN_TM = 512
FFN_FC = 256


def _ffn_ln_kernel(x_ref, wg_ref, wu_ref, wd_ref, g_ref, b_ref, o_ref, h_ref):
    x = x_ref[...]
    xb = x.astype(BF16)
    for c in range(D_FF // FFN_FC):
        sl = slice(c * FFN_FC, (c + 1) * FFN_FC)
        gate = _dot(xb, wg_ref[:, sl])
        up = _dot(xb, wu_ref[:, sl])
        h_ref[:, sl] = (_silu(gate) * up).astype(BF16)
    y = DEEPNORM_ALPHA * x + 0.5 * _dot(h_ref[...], wd_ref[...])
    o_ref[...] = _layer_norm(y, g_ref[...], b_ref[...])


def _ffn_ln(x, wg, wu, wd, g, b):
    n = x.shape[0]
    tm = min(FFN_TM, n)
    row = pl.BlockSpec((tm, D_MODEL), lambda i: (i, 0))
    return pl.pallas_call(
        _ffn_ln_kernel,
        grid=(n // tm,),
        in_specs=[row, _resident(wg.shape), _resident(wu.shape), _resident(wd.shape),
                  _resident(g.shape), _resident(b.shape)],
        out_specs=row,
        out_shape=jax.ShapeDtypeStruct((n, D_MODEL), F32),
        scratch_shapes=[pltpu.VMEM((tm, D_FF), BF16)],
        compiler_params=_params(("parallel",)),
        name="ffn_ln",
    )(x, wg, wu, wd, g, b)


DN_TT = 256
DN_HALO = SUBLANES


def _deltanet_kernel(x_ref, wqkv_ref, wz_ref, wab_ref, cw_ref, alog_ref, dtb_ref, nw_ref, o_ref,
                     buf_ref, q_ref, k_ref, v_ref, z_ref, g_ref, beta_ref, s_ref):
    tt = x_ref.shape[0]
    c = DN_CHUNK

    @pl.when(pl.program_id(1) == 0)
    def _():
        buf_ref[0:DN_HALO, :] = jnp.zeros((DN_HALO, 3 * DN_DIM), F32)
        s_ref[...] = jnp.zeros(s_ref.shape, F32)

    xb = x_ref[...].astype(BF16)
    buf_ref[DN_HALO:DN_HALO + tt, :] = _dot(xb, wqkv_ref[...])
    z_ref[...] = _dot(xb, wz_ref[...])
    ab = _dot(xb, wab_ref[...])

    conv = jnp.zeros((tt, 3 * DN_DIM), F32)
    for j in range(DN_CONV):
        off = DN_HALO - (DN_CONV - 1) + j
        conv = conv + cw_ref[j:j + 1, :] * buf_ref[off:off + tt, :]
    tail = buf_ref[tt:tt + DN_HALO, :]
    buf_ref[0:DN_HALO, :] = tail
    qkv = _silu(conv)

    scale = DN_HEAD_DIM ** -0.5
    for h in range(DN_HEADS):
        sl = slice(h * DN_HEAD_DIM, (h + 1) * DN_HEAD_DIM)
        qh = qkv[:, h * DN_HEAD_DIM:(h + 1) * DN_HEAD_DIM]
        kh = qkv[:, DN_DIM + h * DN_HEAD_DIM:DN_DIM + (h + 1) * DN_HEAD_DIM]
        qh = qh * lax.rsqrt(jnp.sum(qh * qh, axis=-1, keepdims=True) + 1e-6)
        kh = kh * lax.rsqrt(jnp.sum(kh * kh, axis=-1, keepdims=True) + 1e-6)
        q_ref[:, sl] = qh * scale
        k_ref[:, sl] = kh
    v_ref[...] = qkv[:, 2 * DN_DIM:]

    sp_in = ab + dtb_ref[...]
    softplus = jnp.maximum(sp_in, 0.0) + jnp.log(1.0 + jnp.exp(-jnp.abs(sp_in)))
    head_lane = lax.broadcasted_iota(jnp.int32, ab.shape, 1) < DN_HEADS
    g_ref[...] = jnp.where(head_lane, -jnp.exp(alog_ref[...]) * softplus, 0.0)
    beta_ref[...] = _sigmoid(ab)

    row = lax.broadcasted_iota(jnp.int32, (c, c), 0)
    col = lax.broadcasted_iota(jnp.int32, (c, c), 1)
    causal = col <= row
    strict = col < row
    tril = causal.astype(F32)
    eye = (col == row).astype(F32)
    lane = lax.broadcasted_iota(jnp.int32, (c, LANES), 1)

    def chunk_body(ci, carry):
        r0 = pl.multiple_of(ci * c, c)
        rows = pl.ds(r0, c)
        gcs = _dot(tril, g_ref[rows, :], HIGHEST)
        betas = beta_ref[rows, :]
        for h in range(DN_HEADS):
            sl = slice(h * DN_HEAD_DIM, (h + 1) * DN_HEAD_DIM)
            gc = gcs[:, h:h + 1]
            beta = betas[:, DN_HEADS + h:DN_HEADS + h + 1]
            q = q_ref[rows, sl]
            k = k_ref[rows, sl]
            v = v_ref[rows, sl]
            gl_mat = jnp.where(lane == 0, gc, jnp.where(lane == 1, 1.0, 0.0))
            gr_mat = jnp.where(lane == 0, 1.0, jnp.where(lane == 1, -gc, 0.0))
            diff = _dot_nt(gl_mat, gr_mat, HIGHEST)
            decay = jnp.where(causal, jnp.exp(jnp.minimum(diff, 0.0)), 0.0)
            kb = k.astype(BF16)
            kk = _dot_nt(kb, kb)
            qk = _dot_nt(q.astype(BF16), kb)
            lmat = jnp.where(strict, beta * kk * decay, 0.0)
            inv = eye - lmat
            power = lmat
            for _ in range(5):
                power = _dot(power, power, HIGHEST)
                inv = inv + _dot(inv, power, HIGHEST)
            eg = jnp.exp(gc)
            rhs = jnp.concatenate([v * beta, k * (beta * eg)], axis=1)
            sol = _dot(inv, rhs, HIGHEST)
            u = sol[:, :DN_HEAD_DIM]
            w = sol[:, DN_HEAD_DIM:]
            qkm = jnp.where(causal, qk * decay, 0.0)
            g_last = gc[c - 1:c, :]
            q_dec = q * eg
            k_dec = k * jnp.exp(g_last - gc)
            state = s_ref[h]
            sb = state.astype(BF16)
            v_new = u - _dot(w.astype(BF16), sb)
            vb = v_new.astype(BF16)
            o = _dot(q_dec.astype(BF16), sb) + _dot(qkm.astype(BF16), vb)
            s_ref[h] = state * jnp.exp(g_last) + _dot_tn(k_dec.astype(BF16), vb)
            on = o * lax.rsqrt(jnp.mean(o * o, axis=-1, keepdims=True) + NORM_EPS) * nw_ref[...]
            o_ref[rows, sl] = (on * _silu(z_ref[rows, sl])).astype(o_ref.dtype)
        return carry

    lax.fori_loop(0, tt // c, chunk_body, 0)


def _deltanet(x, batch, seq, wqkv, wz, wab, conv_w, a_log, dt_bias, norm_w):
    n = x.shape[0]
    tt = min(DN_TT, seq)
    nt = seq // tt
    row = lambda w: pl.BlockSpec((tt, w), lambda b, t: (b * nt + t, 0))
    consts = (wqkv, wz, wab, conv_w, a_log, dt_bias, norm_w)
    return pl.pallas_call(
        _deltanet_kernel,
        grid=(batch, nt),
        in_specs=[row(D_MODEL)] + [_resident(a.shape) for a in consts],
        out_specs=row(DN_DIM),
        out_shape=jax.ShapeDtypeStruct((n, DN_DIM), BF16),
        scratch_shapes=[
            pltpu.VMEM((DN_HALO + tt, 3 * DN_DIM), F32),
            pltpu.VMEM((tt, DN_DIM), F32),
            pltpu.VMEM((tt, DN_DIM), F32),
            pltpu.VMEM((tt, DN_DIM), F32),
            pltpu.VMEM((tt, DN_DIM), F32),
            pltpu.VMEM((tt, LANES), F32),
            pltpu.VMEM((tt, LANES), F32),
            pltpu.VMEM((DN_HEADS, DN_HEAD_DIM, DN_HEAD_DIM), F32),
        ],
        compiler_params=_params(("parallel", "arbitrary")),
        name="deltanet",
    )(x, *consts)


CV_TT = 256
CV_HALO = 32
CV_RB = 64


def _conv_kernel(x_ref, wglu_ref, glub_ref, dww_ref, dwb_ref, lng_ref, lnb_ref, o_ref, hbuf_ref, cbuf_ref):
    tt = x_ref.shape[0]

    @pl.when(pl.program_id(1) == 0)
    def _():
        hbuf_ref[0:CV_HALO, :] = jnp.zeros((CV_HALO, CV_CHANNELS), F32)

    u = _dot(x_ref[...].astype(BF16), wglu_ref[...]) + glub_ref[...]
    hbuf_ref[CV_HALO:CV_HALO + tt, :] = u[:, :CV_CHANNELS] * _sigmoid(u[:, CV_CHANNELS:])

    base = CV_HALO - (CV_KERNEL - 1)
    for rb in range(tt // CV_RB):
        for s in range(CV_CHANNELS // LANES):
            sl = slice(s * LANES, (s + 1) * LANES)
            acc = jnp.zeros((CV_RB, LANES), F32)
            for j in range(CV_KERNEL):
                r0 = rb * CV_RB + base + j
                acc = acc + dww_ref[j:j + 1, sl] * hbuf_ref[r0:r0 + CV_RB, sl]
            cbuf_ref[rb * CV_RB:(rb + 1) * CV_RB, sl] = acc
    tail = hbuf_ref[tt:tt + CV_HALO, :]
    hbuf_ref[0:CV_HALO, :] = tail

    hc = _layer_norm(cbuf_ref[...] + dwb_ref[...], lng_ref[...], lnb_ref[...])
    o_ref[...] = _silu(hc).astype(o_ref.dtype)


def _conv_branch(x, batch, seq, wglu, glu_b, dw_w, dw_b, ln_g, ln_b):
    n = x.shape[0]
    tt = min(CV_TT, seq)
    nt = seq // tt
    row = lambda w: pl.BlockSpec((tt, w), lambda b, t: (b * nt + t, 0))
    consts = (wglu, glu_b, dw_w, dw_b, ln_g, ln_b)
    return pl.pallas_call(
        _conv_kernel,
        grid=(batch, nt),
        in_specs=[row(D_MODEL)] + [_resident(a.shape) for a in consts],
        out_specs=row(CV_CHANNELS),
        out_shape=jax.ShapeDtypeStruct((n, CV_CHANNELS), BF16),
        scratch_shapes=[pltpu.VMEM((CV_HALO + tt, CV_CHANNELS), F32),
                        pltpu.VMEM((tt, CV_CHANNELS), F32)],
        compiler_params=_params(("parallel", "arbitrary")),
        name="conv_branch",
    )(x, *consts)


ROPE_TM = 512
MLA_TM = 256
ATT_TQ = 256


def _rope_table_kernel(pos_ref, freq_ref, c_ref, s1_ref, s2_ref):
    half = MLA_ROPE // 2
    ang = pos_ref[...] * freq_ref[...]
    lane = lax.broadcasted_iota(jnp.int32, ang.shape, 1)
    cos = jnp.cos(ang)
    sin = jnp.sin(ang)
    c_ref[...] = jnp.where(lane < MLA_ROPE, cos, 0.0)
    s1_ref[...] = jnp.where((lane >= half) & (lane < MLA_ROPE), sin, 0.0)
    s2_ref[...] = jnp.where(lane < half, -sin, 0.0)


def _rope_tables(pos, freq):
    n = pos.shape[0]
    tm = min(ROPE_TM, n)
    out = pl.BlockSpec((tm, LANES), lambda i: (i, 0))
    return pl.pallas_call(
        _rope_table_kernel,
        grid=(n // tm,),
        in_specs=[pl.BlockSpec((tm, 1), lambda i: (i, 0)), _resident(freq.shape)],
        out_specs=[out, out, out],
        out_shape=[jax.ShapeDtypeStruct((n, LANES), F32)] * 3,
        compiler_params=_params(("parallel",)),
        name="rope_tables",
    )(pos, freq)


def _rope(x, c, s1, s2):
    half = MLA_ROPE // 2
    return x * c + pltpu.roll(x, half, 1) * s1 + pltpu.roll(x, LANES - half, 1) * s2


def _mla_prep_kernel(x_ref, wcq_ref, wckv_ref, wkr_ref, qnw_ref, kvnw_ref, wuq_ref, wukv_ref,
                     c_ref, s1_ref, s2_ref, q_out, k_out, v_out):
    xb = x_ref[...].astype(BF16)
    cq = _dot(xb, wcq_ref[...])
    ckv = _dot(xb, wckv_ref[...])
    kr = _dot(xb, wkr_ref[...])
    cqn = cq * lax.rsqrt(jnp.mean(cq * cq, axis=-1, keepdims=True) + NORM_EPS) * qnw_ref[...]
    ckvn = ckv * lax.rsqrt(jnp.mean(ckv * ckv, axis=-1, keepdims=True) + NORM_EPS) * kvnw_ref[...]
    q = _dot(cqn.astype(BF16), wuq_ref[...])
    kv = _dot(ckvn.astype(BF16), wukv_ref[...])
    c, s1, s2 = c_ref[...], s1_ref[...], s2_ref[...]
    kr_rot = _rope(kr, c, s1, s2).astype(BF16)
    for h in range(MLA_HEADS):
        b0 = h * MLA_QK_PAD
        q_out[:, b0:b0 + LANES] = q[:, b0:b0 + LANES].astype(BF16)
        q_out[:, b0 + LANES:b0 + 2 * LANES] = _rope(q[:, b0 + LANES:b0 + 2 * LANES], c, s1, s2).astype(BF16)
        k_out[:, b0:b0 + LANES] = kv[:, b0:b0 + LANES].astype(BF16)
        k_out[:, b0 + LANES:b0 + 2 * LANES] = kr_rot
        v_out[:, h * MLA_V:(h + 1) * MLA_V] = kv[:, b0 + LANES:b0 + 2 * LANES].astype(BF16)


def _mla_prep(x, wcq, wckv, wkr, qnw, kvnw, wuq, wukv, c, s1, s2):
    n = x.shape[0]
    tm = min(MLA_TM, n)
    row = lambda w: pl.BlockSpec((tm, w), lambda i: (i, 0))
    consts = (wcq, wckv, wkr, qnw, kvnw, wuq, wukv)
    return pl.pallas_call(
        _mla_prep_kernel,
        grid=(n // tm,),
        in_specs=[row(D_MODEL)] + [_resident(a.shape) for a in consts] + [row(LANES)] * 3,
        out_specs=[row(MLA_HEADS * MLA_QK_PAD), row(MLA_HEADS * MLA_QK_PAD), row(MLA_HEADS * MLA_V)],
        out_shape=[jax.ShapeDtypeStruct((n, MLA_HEADS * MLA_QK_PAD), BF16),
                   jax.ShapeDtypeStruct((n, MLA_HEADS * MLA_QK_PAD), BF16),
                   jax.ShapeDtypeStruct((n, MLA_HEADS * MLA_V), BF16)],
        compiler_params=_params(("parallel",)),
        name="mla_prep",
    )(x, *consts, c, s1, s2)


def _attention_kernel(q_ref, k_ref, v_ref, o_ref):
    tq = q_ref.shape[0]
    i = pl.program_id(2)
    scale = (MLA_NOPE + MLA_ROPE) ** -0.5
    q = q_ref[...]

    def update(carry, s, vj):
        m, l, acc = carry
        m_new = jnp.maximum(m, jnp.max(s, axis=-1, keepdims=True))
        p = jnp.exp(s - m_new)
        alpha = jnp.exp(m - m_new)
        l_new = alpha * l + jnp.sum(p, axis=-1, keepdims=True)
        acc_new = alpha * acc + _dot(p.astype(BF16), vj)
        return m_new, l_new, acc_new

    def kv_step(j, carry):
        rows = pl.ds(pl.multiple_of(j * tq, tq), tq)
        s = _dot_nt(q, k_ref[rows, :]) * scale
        return update(carry, s, v_ref[rows, :])

    init = (jnp.full((tq, 1), -1e30, F32), jnp.zeros((tq, 1), F32), jnp.zeros((tq, MLA_V), F32))
    carry = lax.fori_loop(0, i, kv_step, init)
    rows = pl.ds(pl.multiple_of(i * tq, tq), tq)
    s = _dot_nt(q, k_ref[rows, :]) * scale
    r = lax.broadcasted_iota(jnp.int32, (tq, tq), 0)
    c = lax.broadcasted_iota(jnp.int32, (tq, tq), 1)
    s = jnp.where(c <= r, s, -1e30)
    m, l, acc = update(carry, s, v_ref[rows, :])
    o_ref[...] = (acc / l).astype(o_ref.dtype)


def _attention(q, k, v, batch, seq):
    n = q.shape[0]
    tq = min(ATT_TQ, seq)
    nq = seq // tq
    return pl.pallas_call(
        _attention_kernel,
        grid=(batch, MLA_HEADS, nq),
        in_specs=[pl.BlockSpec((tq, MLA_QK_PAD), lambda b, h, i: (b * nq + i, h)),
                  pl.BlockSpec((seq, MLA_QK_PAD), lambda b, h, i: (b, h)),
                  pl.BlockSpec((seq, MLA_V), lambda b, h, i: (b, h))],
        out_specs=pl.BlockSpec((tq, MLA_V), lambda b, h, i: (b * nq + i, h)),
        out_shape=jax.ShapeDtypeStruct((n, MLA_HEADS * MLA_V), BF16),
        compiler_params=_params(("parallel", "parallel", "parallel")),
        name="mla_attention",
    )(q, k, v)


MERGE_TM = 256


def _merge_kernel(x_ref, odn_ref, hcv_ref, omla_ref, wgate_ref, bgate_ref, wdn_ref, wcv_ref, bcv_ref,
                  wmla_ref, wout_ref, g_ref, b_ref, o_ref):
    x = x_ref[...]
    gates = _sigmoid(_dot(x.astype(BF16), wgate_ref[...]) + bgate_ref[...])
    y_dn = _dot(odn_ref[...], wdn_ref[...])
    y_cv = _dot(hcv_ref[...], wcv_ref[...]) + bcv_ref[...]
    y_mla = _dot(omla_ref[...], wmla_ref[...])
    merged = (gates[:, :D_MODEL] * y_dn + gates[:, D_MODEL:2 * D_MODEL] * y_cv
              + gates[:, 2 * D_MODEL:] * y_mla)
    y = DEEPNORM_ALPHA * x + _dot(merged.astype(BF16), wout_ref[...])
    o_ref[...] = _layer_norm(y, g_ref[...], b_ref[...])


def _merge(x, odn, hcv, omla, wgate, bgate, wdn, wcv, bcv, wmla, wout, g, b):
    n = x.shape[0]
    tm = min(MERGE_TM, n)
    row = lambda w: pl.BlockSpec((tm, w), lambda i: (i, 0))
    consts = (wgate, bgate, wdn, wcv, bcv, wmla, wout, g, b)
    return pl.pallas_call(
        _merge_kernel,
        grid=(n // tm,),
        in_specs=[row(D_MODEL), row(DN_DIM), row(CV_CHANNELS), row(MLA_HEADS * MLA_V)]
                 + [_resident(a.shape) for a in consts],
        out_specs=row(D_MODEL),
        out_shape=jax.ShapeDtypeStruct((n, D_MODEL), F32),
        compiler_params=_params(("parallel",)),
        name="merge",
    )(x, odn, hcv, omla, *consts)


_IN_SIZES = (DN_DIM, DN_DIM, DN_DIM, DN_DIM, DN_HEADS, DN_HEADS, 2 * CV_CHANNELS,
             MLA_Q_LORA, MLA_KV_LORA, MLA_ROPE, N_BRANCHES * D_MODEL)


def _pad_cols(w, width):
    return jnp.pad(w, ((0, 0), (0, width - w.shape[1])))


def _row(v, width=None):
    v = v.reshape(1, -1).astype(F32)
    return v if width is None else _pad_cols(v, width)


def _split_w_in(w_in):
    offs = np.concatenate([[0], np.cumsum(_IN_SIZES)])
    part = lambda i, j=None: w_in[:, offs[i]:offs[(i if j is None else j) + 1]]
    return dict(
        wqkv=part(0, 2).astype(BF16),
        wz=part(3).astype(BF16),
        wab=_pad_cols(part(4, 5), LANES).astype(BF16),
        wglu=part(6).astype(BF16),
        wcq=part(7).astype(BF16),
        wckv=part(8).astype(BF16),
        wkr=_pad_cols(part(9), LANES).astype(BF16),
        wgate=part(10).astype(BF16),
    )


def _pad_w_uq(w_uq):
    w = w_uq.reshape(MLA_Q_LORA, MLA_HEADS, MLA_NOPE + MLA_ROPE)
    w = jnp.pad(w, ((0, 0), (0, 0), (0, MLA_QK_PAD - MLA_NOPE - MLA_ROPE)))
    return w.reshape(MLA_Q_LORA, MLA_HEADS * MLA_QK_PAD).astype(BF16)


def kernel(x, positions, ln1_g, ln1_b, ffn1_w_gate, ffn1_w_up, ffn1_w_down, w_in, b_gate, dn_conv_w, dn_a_log, dn_dt_bias, dn_norm_w, dn_w_o, cv_glu_b, cv_dw_w, cv_dw_b, cv_ln_g, cv_ln_b, cv_w_pw2, cv_b_pw2, mla_q_norm_w, mla_w_uq, mla_kv_norm_w, mla_w_ukv, mla_w_o, w_out, ln2_g, ln2_b, ffn2_w_gate, ffn2_w_up, ffn2_w_down, ln3_g, ln3_b):
    batch, seq, d = x.shape
    n = batch * seq
    h = x.reshape(n, d)

    half = MLA_ROPE // 2
    inv_freq = ROPE_THETA ** (-np.arange(half, dtype=np.float32) / half)
    freq = np.zeros((1, LANES), np.float32)
    freq[0, :half] = inv_freq
    freq[0, half:MLA_ROPE] = inv_freq
    rope_c, rope_s1, rope_s2 = _rope_tables(positions.astype(F32).reshape(n, 1), jnp.asarray(freq))

    for l in range(DEPTH):
        w = _split_w_in(w_in[l])
        h = _ffn_ln(h, ffn1_w_gate[l].astype(BF16), ffn1_w_up[l].astype(BF16), ffn1_w_down[l].astype(BF16),
                    _row(ln1_g[l]), _row(ln1_b[l]))
        o_dn = _deltanet(h, batch, seq, w["wqkv"], w["wz"], w["wab"], dn_conv_w[l].astype(F32),
                         _row(dn_a_log[l], LANES), _row(dn_dt_bias[l], LANES),
                         _row(dn_norm_w[l]))
        h_cv = _conv_branch(h, batch, seq, w["wglu"], _row(cv_glu_b[l]),
                            jnp.pad(cv_dw_w[l].astype(F32), ((0, CV_HALO - CV_KERNEL), (0, 0))),
                            _row(cv_dw_b[l]), _row(cv_ln_g[l]), _row(cv_ln_b[l]))
        q, k, v = _mla_prep(h, w["wcq"], w["wckv"], w["wkr"], _row(mla_q_norm_w[l]), _row(mla_kv_norm_w[l]),
                            _pad_w_uq(mla_w_uq[l]), mla_w_ukv[l].astype(BF16), rope_c, rope_s1, rope_s2)
        o_mla = _attention(q, k, v, batch, seq)
        h = _merge(h, o_dn, h_cv, o_mla, w["wgate"], _row(b_gate[l]), dn_w_o[l].astype(BF16),
                   cv_w_pw2[l].astype(BF16), _row(cv_b_pw2[l]), mla_w_o[l].astype(BF16),
                   w_out[l].astype(BF16), _row(ln2_g[l]), _row(ln2_b[l]))
        h = _ffn_ln(h, ffn2_w_gate[l].astype(BF16), ffn2_w_up[l].astype(BF16), ffn2_w_down[l].astype(BF16),
                    _row(ln3_g[l]), _row(ln3_b[l]))
    return h.reshape(batch, seq, d)
```

```python
import functools
import math

import jax
import jax.numpy as jnp
import numpy as np
from jax import lax
from jax.experimental import pallas as pl
from jax.experimental.pallas import tpu as pltpu

D_MODEL = 1024
DEPTH = 2
D_FF = 2816
DN_HEADS = 4
DN_HEAD_DIM = 128
DN_DIM = DN_HEADS * DN_HEAD_DIM
DN_CONV = 4
CV_CHANNELS = 512
CV_KERNEL = 31
MLA_HEADS = 4
MLA_Q_LORA = 384
MLA_KV_LORA = 256
MLA_NOPE = 128
MLA_ROPE = 64
MLA_V = 128
ROPE_THETA = 10000.0
N_BRANCHES = 3
DEEPNORM_ALPHA = (2 * DEPTH) ** 0.25
NORM_EPS = 1e-5

LANES = 128
SUBLANES = 8
MLA_QK_PAD = 256
VMEM_LIMIT = 56 * 1024 * 1024

F32 = jnp.float32
BF16 = jnp.bfloat16
HIGHEST = lax.Precision.HIGHEST


def _dot(a, b, precision=None):
    return jnp.dot(a, b, preferred_element_type=F32, precision=precision)


def _dot_nt(a, b, precision=None):
    return lax.dot_general(a, b, (((1,), (1,)), ((), ())), preferred_element_type=F32, precision=precision)


def _dot_tn(a, b):
    return lax.dot_general(a, b, (((0,), (0,)), ((), ())), preferred_element_type=F32)


def _sigmoid(x):
    return 1.0 / (1.0 + jnp.exp(-x))


def _silu(x):
    return x * _sigmoid(x)


def _layer_norm(y, g, b):
    mu = jnp.mean(y, axis=-1, keepdims=True)
    d = y - mu
    var = jnp.mean(d * d, axis=-1, keepdims=True)
    return d * lax.rsqrt(var + NORM_EPS) * g + b


def _resident(shape):
    nd = len(shape)
    return pl.BlockSpec(shape, lambda *_: (0,) * nd, pipeline_mode=pl.Buffered(1))


def _params(sem):
    return pltpu.CompilerParams(dimension_semantics=sem, vmem_limit_bytes=VMEM_LIMIT)


FFN_TM = 512
FFN_FC = 256


def _ffn_ln_kernel(x_ref, wg_ref, wu_ref, wd_ref, g_ref, b_ref, o_ref, h_ref):
    x = x_ref[...]
    xb = x.astype(BF16)
    for c in range(D_FF // FFN_FC):
        sl = slice(c * FFN_FC, (c + 1) * FFN_FC)
        gate = _dot(xb, wg_ref[:, sl])
        up = _dot(xb, wu_ref[:, sl])
        h_ref[:, sl] = (_silu(gate) * up).astype(BF16)
    y = DEEPNORM_ALPHA * x + 0.5 * _dot(h_ref[...], wd_ref[...])
    o_ref[...] = _layer_norm(y, g_ref[...], b_ref[...])


def _ffn_ln(x, wg, wu, wd, g, b):
    n = x.shape[0]
    tm = min(FFN_TM, n)
    row = pl.BlockSpec((tm, D_MODEL), lambda i: (i, 0))
    return pl.pallas_call(
        _ffn_ln_kernel,
        grid=(n // tm,),
        in_specs=[row, _resident(wg.shape), _resident(wu.shape), _resident(wd.shape),
                  _resident(g.shape), _resident(b.shape)],
        out_specs=row,
        out_shape=jax.ShapeDtypeStruct((n, D_MODEL), F32),
        scratch_shapes=[pltpu.VMEM((tm, D_FF), BF16)],
        compiler_params=_params(("parallel",)),
        name="ffn_ln",
    )(x, wg, wu, wd, g, b)


DN_TT = 256
DN_SUB = 64
DN_HALO = SUBLANES


def _deltanet_kernel(x_ref, wqkv_ref, wz_ref, wab_ref, cw_ref, alog_ref, dtb_ref, nw_ref, o_ref,
                     buf_ref, s_ref):
    tt = x_ref.shape[0]

    @pl.when(pl.program_id(1) == 0)
    def _():
        buf_ref[0:DN_HALO, :] = jnp.zeros((DN_HALO, 3 * DN_DIM), F32)
        s_ref[...] = jnp.zeros(s_ref.shape, F32)

    xb = x_ref[...].astype(BF16)
    buf_ref[DN_HALO:DN_HALO + tt, :] = _dot(xb, wqkv_ref[...])
    z = _dot(xb, wz_ref[...])
    ab = _dot(xb, wab_ref[...])

    conv = jnp.zeros((tt, 3 * DN_DIM), F32)
    for j in range(DN_CONV):
        off = DN_HALO - (DN_CONV - 1) + j
        conv = conv + cw_ref[j:j + 1, :] * buf_ref[off:off + tt, :]
    tail = buf_ref[tt:tt + DN_HALO, :]
    buf_ref[0:DN_HALO, :] = tail
    qkv = _silu(conv)

    sp_in = ab + dtb_ref[...]
    softplus = jnp.maximum(sp_in, 0.0) + jnp.log(1.0 + jnp.exp(-jnp.abs(sp_in)))
    head_lane = lax.broadcasted_iota(jnp.int32, ab.shape, 1) < DN_HEADS
    g = jnp.where(head_lane, -jnp.exp(alog_ref[...]) * softplus, 0.0)
    betas = _sigmoid(ab)

    row = lax.broadcasted_iota(jnp.int32, (tt, tt), 0)
    col = lax.broadcasted_iota(jnp.int32, (tt, tt), 1)
    causal = col <= row
    strict = col < row
    same_sub = (row // DN_SUB) == (col // DN_SUB)
    eye = (col == row).astype(F32)
    gcs = _dot(causal.astype(F32), g, HIGHEST)
    gcs_t = gcs.T

    scale = DN_HEAD_DIM ** -0.5
    for h in range(DN_HEADS):
        sl = slice(h * DN_HEAD_DIM, (h + 1) * DN_HEAD_DIM)
        q = qkv[:, h * DN_HEAD_DIM:(h + 1) * DN_HEAD_DIM]
        k = qkv[:, DN_DIM + h * DN_HEAD_DIM:DN_DIM + (h + 1) * DN_HEAD_DIM]
        v = qkv[:, 2 * DN_DIM + h * DN_HEAD_DIM:2 * DN_DIM + (h + 1) * DN_HEAD_DIM]
        q = q * lax.rsqrt(jnp.sum(q * q, axis=-1, keepdims=True) + 1e-6) * scale
        k = k * lax.rsqrt(jnp.sum(k * k, axis=-1, keepdims=True) + 1e-6)
        gc = gcs[:, h:h + 1]
        beta = betas[:, DN_HEADS + h:DN_HEADS + h + 1]
        diff = gc - gcs_t[h:h + 1, :]
        decay = jnp.where(causal, jnp.exp(jnp.minimum(diff, 0.0)), 0.0)
        kb = k.astype(BF16)
        kk = _dot_nt(kb, kb)
        qk = _dot_nt(q.astype(BF16), kb)
        lmat = jnp.where(strict, beta * kk * decay, 0.0)
        l_diag = jnp.where(same_sub, lmat, 0.0)
        inv_d = eye - l_diag
        power = l_diag
        for _ in range(int(math.log2(DN_SUB)) - 1):
            pb = power.astype(BF16)
            power = _dot(pb, pb)
            inv_d = inv_d + _dot(inv_d.astype(BF16), power.astype(BF16))
        inv_db = inv_d.astype(BF16)
        nmat = _dot(inv_db, (lmat - l_diag).astype(BF16))
        inv_n = eye - nmat
        power = nmat
        for _ in range(int(math.log2(tt // DN_SUB)) - 1):
            pb = power.astype(BF16)
            power = _dot(pb, pb)
            inv_n = inv_n + _dot(inv_n.astype(BF16), power.astype(BF16))
        eg = jnp.exp(gc)
        rhs = jnp.concatenate([v * beta, k * (beta * eg)], axis=1)
        sol = _dot(inv_n.astype(BF16), _dot(inv_db, rhs.astype(BF16)).astype(BF16))
        u = sol[:, :DN_HEAD_DIM]
        w = sol[:, DN_HEAD_DIM:]
        qkm = jnp.where(causal, qk * decay, 0.0)
        g_last = gc[tt - 1:tt, :]
        q_dec = q * eg
        k_dec = k * jnp.exp(g_last - gc)
        state = s_ref[h]
        sb = state.astype(BF16)
        v_new = u - _dot(w.astype(BF16), sb)
        vb = v_new.astype(BF16)
        o = _dot(q_dec.astype(BF16), sb) + _dot(qkm.astype(BF16), vb)
        s_ref[h] = state * jnp.exp(g_last) + _dot_tn(k_dec.astype(BF16), vb)
        on = o * lax.rsqrt(jnp.mean(o * o, axis=-1, keepdims=True) + NORM_EPS) * nw_ref[...]
        o_ref[:, sl] = (on * _silu(z[:, sl])).astype(o_ref.dtype)


def _deltanet(x, batch, seq, wqkv, wz, wab, conv_w, a_log, dt_bias, norm_w):
    n = x.shape[0]
    tt = min(DN_TT, seq)
    nt = seq // tt
    row = lambda w: pl.BlockSpec((tt, w), lambda b, t: (b * nt + t, 0))
    consts = (wqkv, wz, wab, conv_w, a_log, dt_bias, norm_w)
    return pl.pallas_call(
        _deltanet_kernel,
        grid=(batch, nt),
        in_specs=[row(D_MODEL)] + [_resident(a.shape) for a in consts],
        out_specs=row(DN_DIM),
        out_shape=jax.ShapeDtypeStruct((n, DN_DIM), BF16),
        scratch_shapes=[
            pltpu.VMEM((DN_HALO + tt, 3 * DN_DIM), F32),
            pltpu.VMEM((DN_HEADS, DN_HEAD_DIM, DN_HEAD_DIM), F32),
        ],
        compiler_params=_params(("parallel", "arbitrary")),
        name="deltanet",
    )(x, *consts)


CV_TT = 256
CV_HALO = 32
CV_RB = 64


def _conv_kernel(x_ref, wglu_ref, glub_ref, dww_ref, dwb_ref, lng_ref, lnb_ref, o_ref, hbuf_ref, cbuf_ref):
    tt = x_ref.shape[0]

    @pl.when(pl.program_id(1) == 0)
    def _():
        hbuf_ref[0:CV_HALO, :] = jnp.zeros((CV_HALO, CV_CHANNELS), F32)

    u = _dot(x_ref[...].astype(BF16), wglu_ref[...]) + glub_ref[...]
    hbuf_ref[CV_HALO:CV_HALO + tt, :] = u[:, :CV_CHANNELS] * _sigmoid(u[:, CV_CHANNELS:])

    base = CV_HALO - (CV_KERNEL - 1)
    for rb in range(tt // CV_RB):
        for s in range(CV_CHANNELS // LANES):
            sl = slice(s * LANES, (s + 1) * LANES)
            acc = jnp.zeros((CV_RB, LANES), F32)
            for j in range(CV_KERNEL):
                r0 = rb * CV_RB + base + j
                acc = acc + dww_ref[j:j + 1, sl] * hbuf_ref[r0:r0 + CV_RB, sl]
            cbuf_ref[rb * CV_RB:(rb + 1) * CV_RB, sl] = acc
    tail = hbuf_ref[tt:tt + CV_HALO, :]
    hbuf_ref[0:CV_HALO, :] = tail

    hc = _layer_norm(cbuf_ref[...] + dwb_ref[...], lng_ref[...], lnb_ref[...])
    o_ref[...] = _silu(hc).astype(o_ref.dtype)


def _conv_branch(x, batch, seq, wglu, glu_b, dw_w, dw_b, ln_g, ln_b):
    n = x.shape[0]
    tt = min(CV_TT, seq)
    nt = seq // tt
    row = lambda w: pl.BlockSpec((tt, w), lambda b, t: (b * nt + t, 0))
    consts = (wglu, glu_b, dw_w, dw_b, ln_g, ln_b)
    return pl.pallas_call(
        _conv_kernel,
        grid=(batch, nt),
        in_specs=[row(D_MODEL)] + [_resident(a.shape) for a in consts],
        out_specs=row(CV_CHANNELS),
        out_shape=jax.ShapeDtypeStruct((n, CV_CHANNELS), BF16),
        scratch_shapes=[pltpu.VMEM((CV_HALO + tt, CV_CHANNELS), F32),
                        pltpu.VMEM((tt, CV_CHANNELS), F32)],
        compiler_params=_params(("parallel", "arbitrary")),
        name="conv_branch",
    )(x, *consts)


ROPE_TM = 512
MLA_TM = 256
ATT_TQ = 256


def _rope_table_kernel(pos_ref, freq_ref, c_ref, s1_ref, s2_ref):
    half = MLA_ROPE // 2
    ang = pos_ref[...] * freq_ref[...]
    lane = lax.broadcasted_iota(jnp.int32, ang.shape, 1)
    cos = jnp.cos(ang)
    sin = jnp.sin(ang)
    c_ref[...] = jnp.where(lane < MLA_ROPE, cos, 0.0)
    s1_ref[...] = jnp.where((lane >= half) & (lane < MLA_ROPE), sin, 0.0)
    s2_ref[...] = jnp.where(lane < half, -sin, 0.0)


def _rope_tables(pos, freq):
    n = pos.shape[0]
    tm = min(ROPE_TM, n)
    out = pl.BlockSpec((tm, LANES), lambda i: (i, 0))
    return pl.pallas_call(
        _rope_table_kernel,
        grid=(n // tm,),
        in_specs=[pl.BlockSpec((tm, 1), lambda i: (i, 0)), _resident(freq.shape)],
        out_specs=[out, out, out],
        out_shape=[jax.ShapeDtypeStruct((n, LANES), F32)] * 3,
        compiler_params=_params(("parallel",)),
        name="rope_tables",
    )(pos, freq)


def _rope(x, c, s1, s2):
    half = MLA_ROPE // 2
    return x * c + pltpu.roll(x, half, 1) * s1 + pltpu.roll(x, LANES - half, 1) * s2


def _mla_prep_kernel(x_ref, wcq_ref, wckv_ref, wkr_ref, qnw_ref, kvnw_ref, wuq_ref, wukv_ref,
                     c_ref, s1_ref, s2_ref, q_out, k_out, v_out):
    xb = x_ref[...].astype(BF16)
    cq = _dot(xb, wcq_ref[...])
    ckv = _dot(xb, wckv_ref[...])
    kr = _dot(xb, wkr_ref[...])
    cqn = cq * lax.rsqrt(jnp.mean(cq * cq, axis=-1, keepdims=True) + NORM_EPS) * qnw_ref[...]
    ckvn = ckv * lax.rsqrt(jnp.mean(ckv * ckv, axis=-1, keepdims=True) + NORM_EPS) * kvnw_ref[...]
    q = _dot(cqn.astype(BF16), wuq_ref[...])
    kv = _dot(ckvn.astype(BF16), wukv_ref[...])
    c, s1, s2 = c_ref[...], s1_ref[...], s2_ref[...]
    kr_rot = _rope(kr, c, s1, s2).astype(BF16)
    for h in range(MLA_HEADS):
        b0 = h * MLA_QK_PAD
        q_out[:, b0:b0 + LANES] = q[:, b0:b0 + LANES].astype(BF16)
        q_out[:, b0 + LANES:b0 + 2 * LANES] = _rope(q[:, b0 + LANES:b0 + 2 * LANES], c, s1, s2).astype(BF16)
        k_out[:, b0:b0 + LANES] = kv[:, b0:b0 + LANES].astype(BF16)
        k_out[:, b0 + LANES:b0 + 2 * LANES] = kr_rot
        v_out[:, h * MLA_V:(h + 1) * MLA_V] = kv[:, b0 + LANES:b0 + 2 * LANES].astype(BF16)


def _mla_prep(x, wcq, wckv, wkr, qnw, kvnw, wuq, wukv, c, s1, s2):
    n = x.shape[0]
    tm = min(MLA_TM, n)
    row = lambda w: pl.BlockSpec((tm, w), lambda i: (i, 0))
    consts = (wcq, wckv, wkr, qnw, kvnw, wuq, wukv)
    return pl.pallas_call(
        _mla_prep_kernel,
        grid=(n // tm,),
        in_specs=[row(D_MODEL)] + [_resident(a.shape) for a in consts] + [row(LANES)] * 3,
        out_specs=[row(MLA_HEADS * MLA_QK_PAD), row(MLA_HEADS * MLA_QK_PAD), row(MLA_HEADS * MLA_V)],
        out_shape=[jax.ShapeDtypeStruct((n, MLA_HEADS * MLA_QK_PAD), BF16),
                   jax.ShapeDtypeStruct((n, MLA_HEADS * MLA_QK_PAD), BF16),
                   jax.ShapeDtypeStruct((n, MLA_HEADS * MLA_V), BF16)],
        compiler_params=_params(("parallel",)),
        name="mla_prep",
    )(x, *consts, c, s1, s2)


def _attention_kernel(q_ref, k_ref, v_ref, o_ref, m_ref, l_ref, acc_ref):
    tq = q_ref.shape[0]
    i = pl.program_id(1)
    scale = (MLA_NOPE + MLA_ROPE) ** -0.5
    m_ref[...] = jnp.full(m_ref.shape, -1e30, F32)
    l_ref[...] = jnp.zeros(l_ref.shape, F32)
    acc_ref[...] = jnp.zeros(acc_ref.shape, F32)
    r = lax.broadcasted_iota(jnp.int32, (tq, tq), 0)
    c = lax.broadcasted_iota(jnp.int32, (tq, tq), 1)

    def kv_block(j, masked):
        rows = pl.ds(pl.multiple_of(j * tq, tq), tq)
        for h in range(MLA_HEADS):
            qk_cols = slice(h * MLA_QK_PAD, (h + 1) * MLA_QK_PAD)
            v_cols = slice(h * MLA_V, (h + 1) * MLA_V)
            s = _dot_nt(q_ref[:, qk_cols], k_ref[rows, qk_cols]) * scale
            if masked:
                s = jnp.where(c <= r, s, -1e30)
            m_prev = m_ref[h]
            m_new = jnp.maximum(m_prev, jnp.max(s, axis=-1, keepdims=True))
            alpha = jnp.exp(m_prev - m_new)
            p = jnp.exp(s - jnp.concatenate([m_new] * (tq // LANES), axis=1))
            l_ref[h] = alpha * l_ref[h] + jnp.sum(p, axis=-1, keepdims=True)
            acc_ref[h] = alpha * acc_ref[h] + _dot(p.astype(BF16), v_ref[rows, v_cols])
            m_ref[h] = m_new

    def kv_step(j, carry):
        kv_block(j, False)
        return carry

    lax.fori_loop(0, i, kv_step, 0)
    kv_block(i, True)
    for h in range(MLA_HEADS):
        o_ref[:, h * MLA_V:(h + 1) * MLA_V] = (acc_ref[h] / l_ref[h]).astype(o_ref.dtype)


def _attention(q, k, v, batch, seq):
    n = q.shape[0]
    tq = min(ATT_TQ, seq)
    nq = seq // tq
    stat = pltpu.VMEM((MLA_HEADS, tq, LANES), F32)
    return pl.pallas_call(
        _attention_kernel,
        grid=(batch, nq),
        in_specs=[pl.BlockSpec((tq, MLA_HEADS * MLA_QK_PAD), lambda b, i: (b * nq + i, 0)),
                  pl.BlockSpec((seq, MLA_HEADS * MLA_QK_PAD), lambda b, i: (b, 0)),
                  pl.BlockSpec((seq, MLA_HEADS * MLA_V), lambda b, i: (b, 0))],
        out_specs=pl.BlockSpec((tq, MLA_HEADS * MLA_V), lambda b, i: (b * nq + i, 0)),
        out_shape=jax.ShapeDtypeStruct((n, MLA_HEADS * MLA_V), BF16),
        scratch_shapes=[stat, stat, pltpu.VMEM((MLA_HEADS, tq, MLA_V), F32)],
        compiler_params=_params(("parallel", "parallel")),
        name="mla_attention",
    )(q, k, v)


MERGE_TM = 256


def _merge_kernel(x_ref, odn_ref, hcv_ref, omla_ref, wgate_ref, bgate_ref, wdn_ref, wcv_ref, bcv_ref,
                  wmla_ref, wout_ref, g_ref, b_ref, o_ref):
    x = x_ref[...]
    gates = _sigmoid(_dot(x.astype(BF16), wgate_ref[...]) + bgate_ref[...])
    y_dn = _dot(odn_ref[...], wdn_ref[...])
    y_cv = _dot(hcv_ref[...], wcv_ref[...]) + bcv_ref[...]
    y_mla = _dot(omla_ref[...], wmla_ref[...])
    merged = (gates[:, :D_MODEL] * y_dn + gates[:, D_MODEL:2 * D_MODEL] * y_cv
              + gates[:, 2 * D_MODEL:] * y_mla)
    y = DEEPNORM_ALPHA * x + _dot(merged.astype(BF16), wout_ref[...])
    o_ref[...] = _layer_norm(y, g_ref[...], b_ref[...])


def _merge(x, odn, hcv, omla, wgate, bgate, wdn, wcv, bcv, wmla, wout, g, b):
    n = x.shape[0]
    tm = min(MERGE_TM, n)
    row = lambda w: pl.BlockSpec((tm, w), lambda i: (i, 0))
    consts = (wgate, bgate, wdn, wcv, bcv, wmla, wout, g, b)
    return pl.pallas_call(
        _merge_kernel,
        grid=(n // tm,),
        in_specs=[row(D_MODEL), row(DN_DIM), row(CV_CHANNELS), row(MLA_HEADS * MLA_V)]
                 + [_resident(a.shape) for a in consts],
        out_specs=row(D_MODEL),
        out_shape=jax.ShapeDtypeStruct((n, D_MODEL), F32),
        compiler_params=_params(("parallel",)),
        name="merge",
    )(x, odn, hcv, omla, *consts)


_IN_SIZES = (DN_DIM, DN_DIM, DN_DIM, DN_DIM, DN_HEADS, DN_HEADS, 2 * CV_CHANNELS,
             MLA_Q_LORA, MLA_KV_LORA, MLA_ROPE, N_BRANCHES * D_MODEL)


def _pad_cols(w, width):
    return jnp.pad(w, ((0, 0), (0, width - w.shape[1])))


def _row(v, width=None):
    v = v.reshape(1, -1).astype(F32)
    return v if width is None else _pad_cols(v, width)


def _split_w_in(w_in):
    offs = np.concatenate([[0], np.cumsum(_IN_SIZES)])
    part = lambda i, j=None: w_in[:, offs[i]:offs[(i if j is None else j) + 1]]
    return dict(
        wqkv=part(0, 2).astype(BF16),
        wz=part(3).astype(BF16),
        wab=_pad_cols(part(4, 5), LANES).astype(BF16),
        wglu=part(6).astype(BF16),
        wcq=part(7).astype(BF16),
        wckv=part(8).astype(BF16),
        wkr=_pad_cols(part(9), LANES).astype(BF16),
        wgate=part(10).astype(BF16),
    )


def _pad_w_uq(w_uq):
    w = w_uq.reshape(MLA_Q_LORA, MLA_HEADS, MLA_NOPE + MLA_ROPE)
    w = jnp.pad(w, ((0, 0), (0, 0), (0, MLA_QK_PAD - MLA_NOPE - MLA_ROPE)))
    return w.reshape(MLA_Q_LORA, MLA_HEADS * MLA_QK_PAD).astype(BF16)


def kernel(x, positions, ln1_g, ln1_b, ffn1_w_gate, ffn1_w_up, ffn1_w_down, w_in, b_gate, dn_conv_w, dn_a_log, dn_dt_bias, dn_norm_w, dn_w_o, cv_glu_b, cv_dw_w, cv_dw_b, cv_ln_g, cv_ln_b, cv_w_pw2, cv_b_pw2, mla_q_norm_w, mla_w_uq, mla_kv_norm_w, mla_w_ukv, mla_w_o, w_out, ln2_g, ln2_b, ffn2_w_gate, ffn2_w_up, ffn2_w_down, ln3_g, ln3_b):
    batch, seq, d = x.shape
    n = batch * seq
    h = x.reshape(n, d)

    half = MLA_ROPE // 2
    inv_freq = ROPE_THETA ** (-np.arange(half, dtype=np.float32) / half)
    freq = np.zeros((1, LANES), np.float32)
    freq[0, :half] = inv_freq
    freq[0, half:MLA_ROPE] = inv_freq
    rope_c, rope_s1, rope_s2 = _rope_tables(positions.astype(F32).reshape(n, 1), jnp.asarray(freq))

    for l in range(DEPTH):
        w = _split_w_in(w_in[l])
        h = _ffn_ln(h, ffn1_w_gate[l].astype(BF16), ffn1_w_up[l].astype(BF16), ffn1_w_down[l].astype(BF16),
                    _row(ln1_g[l]), _row(ln1_b[l]))
        o_dn = _deltanet(h, batch, seq, w["wqkv"], w["wz"], w["wab"], dn_conv_w[l].astype(F32),
                         _row(dn_a_log[l], LANES), _row(dn_dt_bias[l], LANES),
                         _row(dn_norm_w[l]))
        h_cv = _conv_branch(h, batch, seq, w["wglu"], _row(cv_glu_b[l]),
                            jnp.pad(cv_dw_w[l].astype(F32), ((0, CV_HALO - CV_KERNEL), (0, 0))),
                            _row(cv_dw_b[l]), _row(cv_ln_g[l]), _row(cv_ln_b[l]))
        q, k, v = _mla_prep(h, w["wcq"], w["wckv"], w["wkr"], _row(mla_q_norm_w[l]), _row(mla_kv_norm_w[l]),
                            _pad_w_uq(mla_w_uq[l]), mla_w_ukv[l].astype(BF16), rope_c, rope_s1, rope_s2)
        o_mla = _attention(q, k, v, batch, seq)
        h = _merge(h, o_dn, h_cv, o_mla, w["wgate"], _row(b_gate[l]), dn_w_o[l].astype(BF16),
                   cv_w_pw2[l].astype(BF16), _row(cv_b_pw2[l]), mla_w_o[l].astype(BF16),
                   w_out[l].astype(BF16), _row(ln2_g[l]), _row(ln2_b[l]))
        h = _ffn_ln(h, ffn2_w_gate[l].astype(BF16), ffn2_w_up[l].astype(BF16), ffn2_w_down[l].astype(BF16),
                    _row(ln3_g[l]), _row(ln3_b[l]))
    return h.reshape(batch, seq, d)
```

```python
import math

import jax
import jax.numpy as jnp
import numpy as np
from jax import lax
from jax.experimental import pallas as pl
from jax.experimental.pallas import tpu as pltpu

D_MODEL = 1024
DEPTH = 2
D_FF = 2816
DN_HEADS = 4
DN_HEAD_DIM = 128
DN_DIM = DN_HEADS * DN_HEAD_DIM
DN_CONV = 4
CV_CHANNELS = 512
CV_KERNEL = 31
MLA_HEADS = 4
MLA_Q_LORA = 384
MLA_KV_LORA = 256
MLA_NOPE = 128
MLA_ROPE = 64
MLA_V = 128
ROPE_THETA = 10000.0
N_BRANCHES = 3
DEEPNORM_ALPHA = (2 * DEPTH) ** 0.25
NORM_EPS = 1e-5

LANES = 128
SUBLANES = 8
MLA_QK_PAD = 256
VMEM_LIMIT = 56 * 1024 * 1024

F32 = jnp.float32
BF16 = jnp.bfloat16
HIGHEST = lax.Precision.HIGHEST


def _dot(a, b, precision=None):
    return jnp.dot(a, b, preferred_element_type=F32, precision=precision)


def _dot_nt(a, b, precision=None):
    return lax.dot_general(a, b, (((1,), (1,)), ((), ())), preferred_element_type=F32, precision=precision)


def _dot_tn(a, b):
    return lax.dot_general(a, b, (((0,), (0,)), ((), ())), preferred_element_type=F32)


def _sigmoid(x):
    return 1.0 / (1.0 + jnp.exp(-x))


def _silu(x):
    return x * _sigmoid(x)


def _layer_norm(y, g, b):
    mu = jnp.mean(y, axis=-1, keepdims=True)
    d = y - mu
    var = jnp.mean(d * d, axis=-1, keepdims=True)
    return d * lax.rsqrt(var + NORM_EPS) * g + b


def _resident(arr):
    return arr, pl.BlockSpec(arr.shape, lambda *_: (0, 0), pipeline_mode=pl.Buffered(1))


def _layer(arr, l, width=None, col=0):
    width = arr.shape[2] if width is None else width
    return arr, pl.BlockSpec((None, arr.shape[1], width), lambda *_: (l, 0, col), pipeline_mode=pl.Buffered(1))


def _vec(arr, l):
    return _layer(arr.reshape(arr.shape[0], 1, arr.shape[1]), l)


def _params(sem):
    return pltpu.CompilerParams(dimension_semantics=sem, vmem_limit_bytes=VMEM_LIMIT)


FFN_TM = 1024
FFN_FC = 256


def _ffn_ln_kernel(x_ref, wg_ref, wu_ref, wd_ref, g_ref, b_ref, o_ref, h_ref):
    x = x_ref[...]
    xb = x.astype(BF16)
    for c in range(D_FF // FFN_FC):
        sl = slice(c * FFN_FC, (c + 1) * FFN_FC)
        gate = _dot(xb, wg_ref[:, sl])
        up = _dot(xb, wu_ref[:, sl])
        h_ref[:, sl] = (_silu(gate) * up).astype(BF16)
    y = DEEPNORM_ALPHA * x + 0.5 * _dot(h_ref[...], wd_ref[...])
    o_ref[...] = _layer_norm(y, g_ref[...], b_ref[...])


def _ffn_ln(x, consts):
    n = x.shape[0]
    tm = min(FFN_TM, n)
    row = pl.BlockSpec((tm, D_MODEL), lambda i: (i, 0))
    return pl.pallas_call(
        _ffn_ln_kernel,
        grid=(n // tm,),
        in_specs=[row] + [spec for _, spec in consts],
        out_specs=row,
        out_shape=jax.ShapeDtypeStruct((n, D_MODEL), F32),
        scratch_shapes=[pltpu.VMEM((tm, D_FF), BF16)],
        compiler_params=_params(("parallel",)),
        name="ffn_ln",
    )(x, *[a for a, _ in consts])


DN_TT = 256
DN_SUB = 64
DN_HALO = SUBLANES


def _deltanet_kernel(x_ref, wqkv_ref, wz_ref, wab_ref, cw_ref, alog_ref, dtb_ref, nw_ref, o_ref,
                     buf_ref, s_ref):
    tt = x_ref.shape[0]

    @pl.when(pl.program_id(1) == 0)
    def _():
        buf_ref[0:DN_HALO, :] = jnp.zeros((DN_HALO, 3 * DN_DIM), F32)
        s_ref[...] = jnp.zeros(s_ref.shape, F32)

    xb = x_ref[...].astype(BF16)
    buf_ref[DN_HALO:DN_HALO + tt, :] = _dot(xb, wqkv_ref[...])
    z = _dot(xb, wz_ref[...])
    ab = _dot(xb, wab_ref[...])

    conv = jnp.zeros((tt, 3 * DN_DIM), F32)
    for j in range(DN_CONV):
        off = DN_HALO - (DN_CONV - 1) + j
        conv = conv + cw_ref[j:j + 1, :] * buf_ref[off:off + tt, :]
    tail = buf_ref[tt:tt + DN_HALO, :]
    buf_ref[0:DN_HALO, :] = tail
    qkv = _silu(conv)

    sp_in = ab + dtb_ref[...]
    softplus = jnp.maximum(sp_in, 0.0) + jnp.log(1.0 + jnp.exp(-jnp.abs(sp_in)))
    head_lane = lax.broadcasted_iota(jnp.int32, ab.shape, 1) < DN_HEADS
    g = jnp.where(head_lane, -jnp.exp(alog_ref[...]) * softplus, 0.0)
    betas = _sigmoid(ab)

    row = lax.broadcasted_iota(jnp.int32, (tt, tt), 0)
    col = lax.broadcasted_iota(jnp.int32, (tt, tt), 1)
    causal = col <= row
    strict = col < row
    same_sub = (row // DN_SUB) == (col // DN_SUB)
    eye = (col == row).astype(F32)
    gcs = _dot(causal.astype(F32), g, HIGHEST)
    gcs_t = gcs.T

    heads = range(DN_HEADS)
    hsl = [slice(h * DN_HEAD_DIM, (h + 1) * DN_HEAD_DIM) for h in heads]
    scale = DN_HEAD_DIM ** -0.5

    def unit(x):
        return x * lax.rsqrt(jnp.sum(x * x, axis=-1, keepdims=True) + 1e-6)

    q = [unit(qkv[:, h * DN_HEAD_DIM:(h + 1) * DN_HEAD_DIM]) * scale for h in heads]
    k = [unit(qkv[:, DN_DIM + h * DN_HEAD_DIM:DN_DIM + (h + 1) * DN_HEAD_DIM]) for h in heads]
    v = [qkv[:, 2 * DN_DIM + h * DN_HEAD_DIM:2 * DN_DIM + (h + 1) * DN_HEAD_DIM] for h in heads]
    gc = [gcs[:, h:h + 1] for h in heads]
    beta = [betas[:, DN_HEADS + h:DN_HEADS + h + 1] for h in heads]
    decay = [jnp.where(causal, jnp.exp(jnp.minimum(gc[h] - gcs_t[h:h + 1, :], 0.0)), 0.0) for h in heads]
    kb = [k[h].astype(BF16) for h in heads]
    kk = [_dot_nt(kb[h], kb[h]) for h in heads]
    qk = [_dot_nt(q[h].astype(BF16), kb[h]) for h in heads]
    lmat = [jnp.where(strict, beta[h] * kk[h] * decay[h], 0.0) for h in heads]
    l_diag = [jnp.where(same_sub, lmat[h], 0.0) for h in heads]
    inv_d = [eye - l_diag[h] for h in heads]
    power = [l_diag[h].astype(BF16) for h in heads]
    for _ in range(int(math.log2(DN_SUB)) - 1):
        power = [_dot(power[h], power[h]).astype(BF16) for h in heads]
        inv_d = [inv_d[h] + _dot(inv_d[h].astype(BF16), power[h]) for h in heads]
    inv_db = [inv_d[h].astype(BF16) for h in heads]
    nmat = [_dot(inv_db[h], (lmat[h] - l_diag[h]).astype(BF16)) for h in heads]
    inv_n = [eye - nmat[h] for h in heads]
    power = [nmat[h].astype(BF16) for h in heads]
    for _ in range(int(math.log2(tt // DN_SUB)) - 1):
        power = [_dot(power[h], power[h]).astype(BF16) for h in heads]
        inv_n = [inv_n[h] + _dot(inv_n[h].astype(BF16), power[h]) for h in heads]
    eg = [jnp.exp(gc[h]) for h in heads]
    rhs = [jnp.concatenate([v[h] * beta[h], k[h] * (beta[h] * eg[h])], axis=1).astype(BF16) for h in heads]
    half = [_dot(inv_db[h], rhs[h]).astype(BF16) for h in heads]
    sol = [_dot(inv_n[h].astype(BF16), half[h]) for h in heads]
    qkm = [jnp.where(causal, qk[h] * decay[h], 0.0).astype(BF16) for h in heads]
    g_last = [gc[h][tt - 1:tt, :] for h in heads]
    q_dec = [(q[h] * eg[h]).astype(BF16) for h in heads]
    k_dec = [(k[h] * jnp.exp(g_last[h] - gc[h])).astype(BF16) for h in heads]
    state = [s_ref[h] for h in heads]
    sb = [state[h].astype(BF16) for h in heads]
    vb = [(sol[h][:, :DN_HEAD_DIM] - _dot(sol[h][:, DN_HEAD_DIM:].astype(BF16), sb[h])).astype(BF16)
          for h in heads]
    o = [_dot(q_dec[h], sb[h]) + _dot(qkm[h], vb[h]) for h in heads]
    for h in heads:
        s_ref[h] = state[h] * jnp.exp(g_last[h]) + _dot_tn(k_dec[h], vb[h])
    for h in heads:
        on = o[h] * lax.rsqrt(jnp.mean(o[h] * o[h], axis=-1, keepdims=True) + NORM_EPS) * nw_ref[...]
        o_ref[:, hsl[h]] = (on * _silu(z[:, hsl[h]])).astype(o_ref.dtype)


def _deltanet(x, batch, seq, consts):
    n = x.shape[0]
    tt = min(DN_TT, seq)
    nt = seq // tt
    row = lambda w: pl.BlockSpec((tt, w), lambda b, t: (b * nt + t, 0))
    return pl.pallas_call(
        _deltanet_kernel,
        grid=(batch, nt),
        in_specs=[row(D_MODEL)] + [spec for _, spec in consts],
        out_specs=row(DN_DIM),
        out_shape=jax.ShapeDtypeStruct((n, DN_DIM), BF16),
        scratch_shapes=[
            pltpu.VMEM((DN_HALO + tt, 3 * DN_DIM), F32),
            pltpu.VMEM((DN_HEADS, DN_HEAD_DIM, DN_HEAD_DIM), F32),
        ],
        compiler_params=_params(("parallel", "arbitrary")),
        name="deltanet",
    )(x, *[a for a, _ in consts])


CV_TT = 256
CV_HALO = 32
CV_RB = 64


def _conv_kernel(x_ref, wglu_ref, glub_ref, dww_ref, dwb_ref, lng_ref, lnb_ref, o_ref,
                 hbuf_ref, shifted_ref, cbuf_ref):
    tt = x_ref.shape[0]

    @pl.when(pl.program_id(1) == 0)
    def _():
        hbuf_ref[0:CV_HALO, :] = jnp.zeros((CV_HALO, CV_CHANNELS), F32)

    u = _dot(x_ref[...].astype(BF16), wglu_ref[...]) + glub_ref[...]
    hbuf_ref[CV_HALO:CV_HALO + tt, :] = u[:, :CV_CHANNELS] * _sigmoid(u[:, CV_CHANNELS:])

    span = CV_HALO + tt - SUBLANES
    for s in range(1, SUBLANES):
        shifted_ref[s - 1] = hbuf_ref[s:s + span, :]

    base = CV_HALO - (CV_KERNEL - 1)
    for rb in range(tt // CV_RB):
        for s in range(CV_CHANNELS // LANES):
            sl = slice(s * LANES, (s + 1) * LANES)
            acc = jnp.zeros((CV_RB, LANES), F32)
            for j in range(CV_KERNEL):
                shift = (base + j) % SUBLANES
                r0 = rb * CV_RB + base + j - shift
                src = hbuf_ref if shift == 0 else shifted_ref.at[shift - 1]
                acc = acc + dww_ref[j:j + 1, sl] * src[r0:r0 + CV_RB, sl]
            cbuf_ref[rb * CV_RB:(rb + 1) * CV_RB, sl] = acc
    tail = hbuf_ref[tt:tt + CV_HALO, :]
    hbuf_ref[0:CV_HALO, :] = tail

    hc = _layer_norm(cbuf_ref[...] + dwb_ref[...], lng_ref[...], lnb_ref[...])
    o_ref[...] = _silu(hc).astype(o_ref.dtype)


def _conv_branch(x, batch, seq, consts):
    n = x.shape[0]
    tt = min(CV_TT, seq)
    nt = seq // tt
    row = lambda w: pl.BlockSpec((tt, w), lambda b, t: (b * nt + t, 0))
    return pl.pallas_call(
        _conv_kernel,
        grid=(batch, nt),
        in_specs=[row(D_MODEL)] + [spec for _, spec in consts],
        out_specs=row(CV_CHANNELS),
        out_shape=jax.ShapeDtypeStruct((n, CV_CHANNELS), BF16),
        scratch_shapes=[pltpu.VMEM((CV_HALO + tt, CV_CHANNELS), F32),
                        pltpu.VMEM((SUBLANES - 1, CV_HALO + tt - SUBLANES, CV_CHANNELS), F32),
                        pltpu.VMEM((tt, CV_CHANNELS), F32)],
        compiler_params=_params(("parallel", "arbitrary")),
        name="conv_branch",
    )(x, *[a for a, _ in consts])


ROPE_TM = 512
MLA_TM = 256
ATT_TQ = 256


def _rope_table_kernel(pos_ref, freq_ref, c_ref, s1_ref, s2_ref):
    half = MLA_ROPE // 2
    ang = pos_ref[...] * freq_ref[...]
    lane = lax.broadcasted_iota(jnp.int32, ang.shape, 1)
    cos = jnp.cos(ang)
    sin = jnp.sin(ang)
    c_ref[...] = jnp.where(lane < MLA_ROPE, cos, 0.0)
    s1_ref[...] = jnp.where((lane >= half) & (lane < MLA_ROPE), sin, 0.0)
    s2_ref[...] = jnp.where(lane < half, -sin, 0.0)


def _rope_tables(pos, freq):
    n = pos.shape[0]
    tm = min(ROPE_TM, n)
    out = pl.BlockSpec((tm, LANES), lambda i: (i, 0))
    return pl.pallas_call(
        _rope_table_kernel,
        grid=(n // tm,),
        in_specs=[pl.BlockSpec((tm, 1), lambda i: (i, 0)), _resident(freq)[1]],
        out_specs=[out, out, out],
        out_shape=[jax.ShapeDtypeStruct((n, LANES), F32)] * 3,
        compiler_params=_params(("parallel",)),
        name="rope_tables",
    )(pos, freq)


def _rope(x, c, s1, s2):
    half = MLA_ROPE // 2
    return x * c + pltpu.roll(x, half, 1) * s1 + pltpu.roll(x, LANES - half, 1) * s2


def _mla_prep_kernel(x_ref, wlat_ref, qnw_ref, kvnw_ref, wuq_ref, wukv_ref,
                     c_ref, s1_ref, s2_ref, q_out, k_out, v_out):
    lat = _dot(x_ref[...].astype(BF16), wlat_ref[...])
    cq = lat[:, :MLA_Q_LORA]
    ckv = lat[:, MLA_Q_LORA:MLA_Q_LORA + MLA_KV_LORA]
    kr = lat[:, MLA_Q_LORA + MLA_KV_LORA:]
    cqn = cq * lax.rsqrt(jnp.mean(cq * cq, axis=-1, keepdims=True) + NORM_EPS) * qnw_ref[...]
    ckvn = ckv * lax.rsqrt(jnp.mean(ckv * ckv, axis=-1, keepdims=True) + NORM_EPS) * kvnw_ref[...]
    q = _dot(cqn.astype(BF16), wuq_ref[...])
    kv = _dot(ckvn.astype(BF16), wukv_ref[...])
    c, s1, s2 = c_ref[...], s1_ref[...], s2_ref[...]
    kr_rot = _rope(kr, c, s1, s2).astype(BF16)
    for h in range(MLA_HEADS):
        b0 = h * MLA_QK_PAD
        q_out[:, b0:b0 + LANES] = q[:, b0:b0 + LANES].astype(BF16)
        q_out[:, b0 + LANES:b0 + 2 * LANES] = _rope(q[:, b0 + LANES:b0 + 2 * LANES], c, s1, s2).astype(BF16)
        k_out[:, b0:b0 + LANES] = kv[:, b0:b0 + LANES].astype(BF16)
        k_out[:, b0 + LANES:b0 + 2 * LANES] = kr_rot
        v_out[:, h * MLA_V:(h + 1) * MLA_V] = kv[:, b0 + LANES:b0 + 2 * LANES].astype(BF16)


def _mla_prep(x, consts, c, s1, s2):
    n = x.shape[0]
    tm = min(MLA_TM, n)
    row = lambda w: pl.BlockSpec((tm, w), lambda i: (i, 0))
    return pl.pallas_call(
        _mla_prep_kernel,
        grid=(n // tm,),
        in_specs=[row(D_MODEL)] + [spec for _, spec in consts] + [row(LANES)] * 3,
        out_specs=[row(MLA_HEADS * MLA_QK_PAD), row(MLA_HEADS * MLA_QK_PAD), row(MLA_HEADS * MLA_V)],
        out_shape=[jax.ShapeDtypeStruct((n, MLA_HEADS * MLA_QK_PAD), BF16),
                   jax.ShapeDtypeStruct((n, MLA_HEADS * MLA_QK_PAD), BF16),
                   jax.ShapeDtypeStruct((n, MLA_HEADS * MLA_V), BF16)],
        compiler_params=_params(("parallel",)),
        name="mla_prep",
    )(x, *[a for a, _ in consts], c, s1, s2)


def _attention_kernel(q_ref, k_ref, v_ref, o_ref, m_ref, l_ref, acc_ref):
    tq = q_ref.shape[0]
    i = pl.program_id(1)
    scale = (MLA_NOPE + MLA_ROPE) ** -0.5
    m_ref[...] = jnp.full(m_ref.shape, -1e30, F32)
    l_ref[...] = jnp.zeros(l_ref.shape, F32)
    acc_ref[...] = jnp.zeros(acc_ref.shape, F32)
    r = lax.broadcasted_iota(jnp.int32, (tq, tq), 0)
    c = lax.broadcasted_iota(jnp.int32, (tq, tq), 1)

    def kv_block(j, masked):
        rows = pl.ds(pl.multiple_of(j * tq, tq), tq)
        heads = range(MLA_HEADS)
        qk_cols = [slice(h * MLA_QK_PAD, (h + 1) * MLA_QK_PAD) for h in heads]
        v_cols = [slice(h * MLA_V, (h + 1) * MLA_V) for h in heads]
        s = [_dot_nt(q_ref[:, qk_cols[h]], k_ref[rows, qk_cols[h]]) * scale for h in heads]
        if masked:
            s = [jnp.where(c <= r, s[h], -1e30) for h in heads]
        m_prev = [m_ref[h] for h in heads]
        m_new = [jnp.maximum(m_prev[h], jnp.max(s[h], axis=-1, keepdims=True)) for h in heads]
        p = [jnp.exp(s[h] - jnp.concatenate([m_new[h]] * (tq // LANES), axis=1)) for h in heads]
        alpha = [jnp.exp(m_prev[h] - m_new[h]) for h in heads]
        pv = [_dot(p[h].astype(BF16), v_ref[rows, v_cols[h]]) for h in heads]
        for h in heads:
            l_ref[h] = alpha[h] * l_ref[h] + jnp.sum(p[h], axis=-1, keepdims=True)
            acc_ref[h] = alpha[h] * acc_ref[h] + pv[h]
            m_ref[h] = m_new[h]

    def kv_step(j, carry):
        kv_block(j, False)
        return carry

    lax.fori_loop(0, i, kv_step, 0)
    kv_block(i, True)
    for h in range(MLA_HEADS):
        o_ref[:, h * MLA_V:(h + 1) * MLA_V] = (acc_ref[h] / l_ref[h]).astype(o_ref.dtype)


def _attention(q, k, v, batch, seq):
    n = q.shape[0]
    tq = min(ATT_TQ, seq)
    nq = seq // tq
    stat = pltpu.VMEM((MLA_HEADS, tq, LANES), F32)
    return pl.pallas_call(
        _attention_kernel,
        grid=(batch, nq),
        in_specs=[pl.BlockSpec((tq, MLA_HEADS * MLA_QK_PAD), lambda b, i: (b * nq + i, 0)),
                  pl.BlockSpec((seq, MLA_HEADS * MLA_QK_PAD), lambda b, i: (b, 0)),
                  pl.BlockSpec((seq, MLA_HEADS * MLA_V), lambda b, i: (b, 0))],
        out_specs=pl.BlockSpec((tq, MLA_HEADS * MLA_V), lambda b, i: (b * nq + i, 0)),
        out_shape=jax.ShapeDtypeStruct((n, MLA_HEADS * MLA_V), BF16),
        scratch_shapes=[stat, stat, pltpu.VMEM((MLA_HEADS, tq, MLA_V), F32)],
        compiler_params=_params(("parallel", "parallel")),
        name="mla_attention",
    )(q, k, v)


MERGE_TM = 512
MERGE_FC = 256


def _merge_kernel(x_ref, odn_ref, hcv_ref, omla_ref, wg_dn_ref, wg_cv_ref, wg_mla_ref, bgate_ref,
                  wdn_ref, wcv_ref, bcv_ref, wmla_ref, wout_ref, g_ref, b_ref, o_ref, merged_ref):
    x = x_ref[...]
    xb = x.astype(BF16)
    branches = ((odn_ref, wdn_ref, None, wg_dn_ref), (hcv_ref, wcv_ref, bcv_ref, wg_cv_ref),
                (omla_ref, wmla_ref, None, wg_mla_ref))
    for c in range(D_MODEL // MERGE_FC):
        sl = slice(c * MERGE_FC, (c + 1) * MERGE_FC)
        merged = None
        for i, (a_ref, w_ref, bias_ref, wgate_ref) in enumerate(branches):
            gsl = slice(i * D_MODEL + c * MERGE_FC, i * D_MODEL + (c + 1) * MERGE_FC)
            gate = _sigmoid(_dot(xb, wgate_ref[:, sl]) + bgate_ref[:, gsl])
            y = _dot(a_ref[...], w_ref[:, sl])
            if bias_ref is not None:
                y = y + bias_ref[:, sl]
            merged = gate * y if merged is None else merged + gate * y
        merged_ref[:, sl] = merged.astype(BF16)
    y = DEEPNORM_ALPHA * x + _dot(merged_ref[...], wout_ref[...])
    o_ref[...] = _layer_norm(y, g_ref[...], b_ref[...])


def _merge(x, odn, hcv, omla, consts):
    n = x.shape[0]
    tm = min(MERGE_TM, n)
    row = lambda w: pl.BlockSpec((tm, w), lambda i: (i, 0))
    return pl.pallas_call(
        _merge_kernel,
        grid=(n // tm,),
        in_specs=[row(D_MODEL), row(DN_DIM), row(CV_CHANNELS), row(MLA_HEADS * MLA_V)]
                 + [spec for _, spec in consts],
        out_specs=row(D_MODEL),
        out_shape=jax.ShapeDtypeStruct((n, D_MODEL), F32),
        scratch_shapes=[pltpu.VMEM((tm, D_MODEL), BF16)],
        compiler_params=_params(("parallel",)),
        name="merge",
    )(x, odn, hcv, omla, *[a for a, _ in consts])


_IN_SIZES = (DN_DIM, DN_DIM, DN_DIM, DN_DIM, DN_HEADS, DN_HEADS, 2 * CV_CHANNELS,
             MLA_Q_LORA, MLA_KV_LORA, MLA_ROPE, N_BRANCHES * D_MODEL)


_IN_OFFS = tuple(int(v) for v in np.concatenate([[0], np.cumsum(_IN_SIZES)]))
_DN_COLS = _IN_OFFS[4] + LANES


def _pad_lanes(v, width):
    return jnp.pad(v.astype(F32), ((0, 0), (0, width - v.shape[1])))


def _split_w_in(w_in):
    part = lambda i, j=None: w_in[:, :, _IN_OFFS[i]:_IN_OFFS[(i if j is None else j) + 1]]
    pad = jnp.zeros(w_in.shape[:2] + (LANES - MLA_ROPE,), w_in.dtype)
    return dict(
        dn=w_in[:, :, :_DN_COLS].astype(BF16),
        latent=jnp.concatenate([part(7, 9), pad], axis=2).astype(BF16),
        wide=jnp.concatenate([part(6), part(10)], axis=2).astype(BF16),
    )


def _pad_w_uq(w_uq):
    depth = w_uq.shape[0]
    w = w_uq.reshape(depth, MLA_Q_LORA, MLA_HEADS, MLA_NOPE + MLA_ROPE)
    w = jnp.pad(w, ((0, 0), (0, 0), (0, 0), (0, MLA_QK_PAD - MLA_NOPE - MLA_ROPE)))
    return w.reshape(depth, MLA_Q_LORA, MLA_HEADS * MLA_QK_PAD).astype(BF16)


def kernel(x, positions, ln1_g, ln1_b, ffn1_w_gate, ffn1_w_up, ffn1_w_down, w_in, b_gate, dn_conv_w, dn_a_log, dn_dt_bias, dn_norm_w, dn_w_o, cv_glu_b, cv_dw_w, cv_dw_b, cv_ln_g, cv_ln_b, cv_w_pw2, cv_b_pw2, mla_q_norm_w, mla_w_uq, mla_kv_norm_w, mla_w_ukv, mla_w_o, w_out, ln2_g, ln2_b, ffn2_w_gate, ffn2_w_up, ffn2_w_down, ln3_g, ln3_b):
    batch, seq, d = x.shape
    n = batch * seq
    h = x.reshape(n, d)

    half = MLA_ROPE // 2
    inv_freq = ROPE_THETA ** (-np.arange(half, dtype=np.float32) / half)
    freq = np.zeros((1, LANES), np.float32)
    freq[0, :half] = inv_freq
    freq[0, half:MLA_ROPE] = inv_freq
    rope_c, rope_s1, rope_s2 = _rope_tables(positions.astype(F32).reshape(n, 1), jnp.asarray(freq))

    bf = lambda a: a.astype(BF16)
    w = _split_w_in(w_in)
    ffn1 = (bf(ffn1_w_gate), bf(ffn1_w_up), bf(ffn1_w_down))
    ffn2 = (bf(ffn2_w_gate), bf(ffn2_w_up), bf(ffn2_w_down))
    dn_w_o_b, cv_w_pw2_b, mla_w_o_b, w_out_b = bf(dn_w_o), bf(cv_w_pw2), bf(mla_w_o), bf(w_out)
    w_uq_b, w_ukv_b = _pad_w_uq(mla_w_uq), bf(mla_w_ukv)
    a_log_p, dt_bias_p = _pad_lanes(dn_a_log, LANES), _pad_lanes(dn_dt_bias, LANES)
    dn_conv_w, cv_dw_w = dn_conv_w.astype(F32), cv_dw_w.astype(F32)

    for l in range(DEPTH):
        h = _ffn_ln(h, [_layer(a, l) for a in ffn1] + [_vec(ln1_g, l), _vec(ln1_b, l)])
        o_dn = _deltanet(h, batch, seq, [
            _layer(w["dn"], l, 3 * DN_DIM, 0), _layer(w["dn"], l, DN_DIM, 3),
            _layer(w["dn"], l, LANES, _IN_OFFS[4] // LANES), _layer(dn_conv_w, l),
            _vec(a_log_p, l), _vec(dt_bias_p, l), _vec(dn_norm_w, l)])
        h_cv = _conv_branch(h, batch, seq, [
            _layer(w["wide"], l, 2 * CV_CHANNELS, 0), _vec(cv_glu_b, l), _layer(cv_dw_w, l),
            _vec(cv_dw_b, l), _vec(cv_ln_g, l), _vec(cv_ln_b, l)])
        q, k, v = _mla_prep(h, [
            _layer(w["latent"], l), _vec(mla_q_norm_w, l), _vec(mla_kv_norm_w, l),
            _layer(w_uq_b, l), _layer(w_ukv_b, l)], rope_c, rope_s1, rope_s2)
        o_mla = _attention(q, k, v, batch, seq)
        h = _merge(h, o_dn, h_cv, o_mla, [
            _layer(w["wide"], l, D_MODEL, 1), _layer(w["wide"], l, D_MODEL, 2), _layer(w["wide"], l, D_MODEL, 3),
            _vec(b_gate, l), _layer(dn_w_o_b, l), _layer(cv_w_pw2_b, l), _vec(cv_b_pw2, l),
            _layer(mla_w_o_b, l), _layer(w_out_b, l), _vec(ln2_g, l), _vec(ln2_b, l)])
        h = _ffn_ln(h, [_layer(a, l) for a in ffn2] + [_vec(ln3_g, l), _vec(ln3_b, l)])
    return h.reshape(batch, seq, d)
```

```python
import math

import jax
import jax.numpy as jnp
import numpy as np
from jax import lax
from jax.experimental import pallas as pl
from jax.experimental.pallas import tpu as pltpu

D_MODEL = 1024
DEPTH = 2
D_FF = 2816
DN_HEADS = 4
DN_HEAD_DIM = 128
DN_DIM = DN_HEADS * DN_HEAD_DIM
DN_CONV = 4
CV_CHANNELS = 512
CV_KERNEL = 31
MLA_HEADS = 4
MLA_Q_LORA = 384
MLA_KV_LORA = 256
MLA_NOPE = 128
MLA_ROPE = 64
MLA_V = 128
ROPE_THETA = 10000.0
N_BRANCHES = 3
DEEPNORM_ALPHA = (2 * DEPTH) ** 0.25
NORM_EPS = 1e-5

LANES = 128
SUBLANES = 8
MLA_QK_PAD = 256
VMEM_LIMIT = 56 * 1024 * 1024

F32 = jnp.float32
BF16 = jnp.bfloat16
HIGHEST = lax.Precision.HIGHEST


def _dot(a, b, precision=None):
    return jnp.dot(a, b, preferred_element_type=F32, precision=precision)


def _dot_nt(a, b, precision=None):
    return lax.dot_general(a, b, (((1,), (1,)), ((), ())), preferred_element_type=F32, precision=precision)


def _dot_tn(a, b):
    return lax.dot_general(a, b, (((0,), (0,)), ((), ())), preferred_element_type=F32)


def _sigmoid(x):
    return 1.0 / (1.0 + jnp.exp(-x))


def _silu(x):
    return x * _sigmoid(x)


def _layer_norm(y, g, b):
    mu = jnp.mean(y, axis=-1, keepdims=True)
    d = y - mu
    var = jnp.mean(d * d, axis=-1, keepdims=True)
    return d * lax.rsqrt(var + NORM_EPS) * g + b


def _resident(arr):
    return arr, pl.BlockSpec(arr.shape, lambda *_: (0, 0), pipeline_mode=pl.Buffered(1))


def _layer(arr, l, width=None, col=0):
    width = arr.shape[2] if width is None else width
    return arr, pl.BlockSpec((None, arr.shape[1], width), lambda *_: (l, 0, col), pipeline_mode=pl.Buffered(1))


def _vec(arr, l):
    return _layer(arr.reshape(arr.shape[0], 1, arr.shape[1]), l)


def _params(sem):
    return pltpu.CompilerParams(dimension_semantics=sem, vmem_limit_bytes=VMEM_LIMIT)


FFN_TM = 1024
FFN_FC = 256


def _ffn_ln_kernel(x_ref, wg_ref, wu_ref, wd_ref, g_ref, b_ref, o_ref, h_ref):
    x = x_ref[...]
    xb = x.astype(BF16)
    for c in range(D_FF // FFN_FC):
        sl = slice(c * FFN_FC, (c + 1) * FFN_FC)
        gate = _dot(xb, wg_ref[:, sl])
        up = _dot(xb, wu_ref[:, sl])
        h_ref[:, sl] = (_silu(gate) * up).astype(BF16)
    y = DEEPNORM_ALPHA * x + 0.5 * _dot(h_ref[...], wd_ref[...])
    o_ref[...] = _layer_norm(y, g_ref[...], b_ref[...])


def _ffn_ln(x, consts):
    n = x.shape[0]
    tm = min(FFN_TM, n)
    row = pl.BlockSpec((tm, D_MODEL), lambda i: (i, 0))
    return pl.pallas_call(
        _ffn_ln_kernel,
        grid=(n // tm,),
        in_specs=[row] + [spec for _, spec in consts],
        out_specs=row,
        out_shape=jax.ShapeDtypeStruct((n, D_MODEL), F32),
        scratch_shapes=[pltpu.VMEM((tm, D_FF), BF16)],
        compiler_params=_params(("parallel",)),
        name="ffn_ln",
    )(x, *[a for a, _ in consts])


DN_TT = 256
DN_SUB = 64
DN_HALO = SUBLANES
DN_NB = 2


def _deltanet_kernel(x_ref, wqkv_ref, wz_ref, wab_ref, cw_ref, alog_ref, dtb_ref, nw_ref, o_ref,
                     buf_ref, s_ref):
    nb, tt = x_ref.shape[0], x_ref.shape[1]
    seqs = range(nb)

    @pl.when(pl.program_id(1) == 0)
    def _():
        buf_ref[:, 0:DN_HALO, :] = jnp.zeros((nb, DN_HALO, 3 * DN_DIM), F32)
        s_ref[...] = jnp.zeros(s_ref.shape, F32)

    xb = [x_ref[b].astype(BF16) for b in seqs]
    for b in seqs:
        buf_ref[b, DN_HALO:DN_HALO + tt, :] = _dot(xb[b], wqkv_ref[...])
    z = [_dot(xb[b], wz_ref[...]) for b in seqs]
    ab = [_dot(xb[b], wab_ref[...]) for b in seqs]

    qkv = []
    for b in seqs:
        conv = jnp.zeros((tt, 3 * DN_DIM), F32)
        for j in range(DN_CONV):
            off = DN_HALO - (DN_CONV - 1) + j
            conv = conv + cw_ref[j:j + 1, :] * buf_ref[b, off:off + tt, :]
        tail = buf_ref[b, tt:tt + DN_HALO, :]
        buf_ref[b, 0:DN_HALO, :] = tail
        qkv.append(_silu(conv))

    row = lax.broadcasted_iota(jnp.int32, (tt, tt), 0)
    col = lax.broadcasted_iota(jnp.int32, (tt, tt), 1)
    causal = col <= row
    strict = col < row
    same_sub = (row // DN_SUB) == (col // DN_SUB)
    eye = (col == row).astype(F32)
    tril = causal.astype(F32)
    head_lane = lax.broadcasted_iota(jnp.int32, (tt, LANES), 1) < DN_HEADS

    betas, gcs, gcs_t = [], [], []
    for b in seqs:
        sp_in = ab[b] + dtb_ref[...]
        softplus = jnp.maximum(sp_in, 0.0) + jnp.log(1.0 + jnp.exp(-jnp.abs(sp_in)))
        g = jnp.where(head_lane, -jnp.exp(alog_ref[...]) * softplus, 0.0)
        betas.append(_sigmoid(ab[b]))
        gcs.append(_dot(tril, g, HIGHEST))
        gcs_t.append(gcs[b].T)

    pairs = [(b, h) for b in seqs for h in range(DN_HEADS)]
    heads = range(len(pairs))
    hsl = [slice(h * DN_HEAD_DIM, (h + 1) * DN_HEAD_DIM) for _, h in pairs]
    scale = DN_HEAD_DIM ** -0.5

    def unit(x):
        return x * lax.rsqrt(jnp.sum(x * x, axis=-1, keepdims=True) + 1e-6)

    q = [unit(qkv[b][:, h * DN_HEAD_DIM:(h + 1) * DN_HEAD_DIM]) * scale for b, h in pairs]
    k = [unit(qkv[b][:, DN_DIM + h * DN_HEAD_DIM:DN_DIM + (h + 1) * DN_HEAD_DIM]) for b, h in pairs]
    v = [qkv[b][:, 2 * DN_DIM + h * DN_HEAD_DIM:2 * DN_DIM + (h + 1) * DN_HEAD_DIM] for b, h in pairs]
    gc = [gcs[b][:, h:h + 1] for b, h in pairs]
    beta = [betas[b][:, DN_HEADS + h:DN_HEADS + h + 1] for b, h in pairs]
    decay = [jnp.where(causal, jnp.exp(jnp.minimum(gc[u] - gcs_t[b][h:h + 1, :], 0.0)), 0.0)
             for u, (b, h) in enumerate(pairs)]
    kb = [k[h].astype(BF16) for h in heads]
    kk = [_dot_nt(kb[h], kb[h]) for h in heads]
    qk = [_dot_nt(q[h].astype(BF16), kb[h]) for h in heads]
    lmat = [jnp.where(strict, beta[h] * kk[h] * decay[h], 0.0) for h in heads]
    l_diag = [jnp.where(same_sub, lmat[h], 0.0) for h in heads]
    inv_d = [eye - l_diag[h] for h in heads]
    power = [l_diag[h].astype(BF16) for h in heads]
    for _ in range(int(math.log2(DN_SUB)) - 1):
        power = [_dot(power[h], power[h]).astype(BF16) for h in heads]
        inv_d = [inv_d[h] + _dot(inv_d[h].astype(BF16), power[h]) for h in heads]
    inv_db = [inv_d[h].astype(BF16) for h in heads]
    nmat = [_dot(inv_db[h], (lmat[h] - l_diag[h]).astype(BF16)) for h in heads]
    inv_n = [eye - nmat[h] for h in heads]
    power = [nmat[h].astype(BF16) for h in heads]
    for _ in range(int(math.log2(tt // DN_SUB)) - 1):
        power = [_dot(power[h], power[h]).astype(BF16) for h in heads]
        inv_n = [inv_n[h] + _dot(inv_n[h].astype(BF16), power[h]) for h in heads]
    eg = [jnp.exp(gc[h]) for h in heads]
    rhs = [jnp.concatenate([v[h] * beta[h], k[h] * (beta[h] * eg[h])], axis=1).astype(BF16) for h in heads]
    half = [_dot(inv_db[h], rhs[h]).astype(BF16) for h in heads]
    sol = [_dot(inv_n[h].astype(BF16), half[h]) for h in heads]
    qkm = [jnp.where(causal, qk[h] * decay[h], 0.0).astype(BF16) for h in heads]
    g_last = [gc[h][tt - 1:tt, :] for h in heads]
    q_dec = [(q[h] * eg[h]).astype(BF16) for h in heads]
    k_dec = [(k[h] * jnp.exp(g_last[h] - gc[h])).astype(BF16) for h in heads]
    state = [s_ref[h] for h in heads]
    sb = [state[h].astype(BF16) for h in heads]
    vb = [(sol[h][:, :DN_HEAD_DIM] - _dot(sol[h][:, DN_HEAD_DIM:].astype(BF16), sb[h])).astype(BF16)
          for h in heads]
    o = [_dot(q_dec[h], sb[h]) + _dot(qkm[h], vb[h]) for h in heads]
    for h in heads:
        s_ref[h] = state[h] * jnp.exp(g_last[h]) + _dot_tn(k_dec[h], vb[h])
    for u, (b, _) in enumerate(pairs):
        on = o[u] * lax.rsqrt(jnp.mean(o[u] * o[u], axis=-1, keepdims=True) + NORM_EPS) * nw_ref[...]
        o_ref[b, :, hsl[u]] = (on * _silu(z[b][:, hsl[u]])).astype(o_ref.dtype)


def _deltanet(x, batch, seq, consts):
    tt = min(DN_TT, seq)
    nb = DN_NB if batch % DN_NB == 0 else 1
    block = lambda w: pl.BlockSpec((nb, tt, w), lambda i, t: (i, t, 0))
    out = pl.pallas_call(
        _deltanet_kernel,
        grid=(batch // nb, seq // tt),
        in_specs=[block(D_MODEL)] + [spec for _, spec in consts],
        out_specs=block(DN_DIM),
        out_shape=jax.ShapeDtypeStruct((batch, seq, DN_DIM), BF16),
        scratch_shapes=[
            pltpu.VMEM((nb, DN_HALO + tt, 3 * DN_DIM), F32),
            pltpu.VMEM((nb * DN_HEADS, DN_HEAD_DIM, DN_HEAD_DIM), F32),
        ],
        compiler_params=_params(("parallel", "arbitrary")),
        name="deltanet",
    )(x.reshape(batch, seq, D_MODEL), *[a for a, _ in consts])
    return out.reshape(batch * seq, DN_DIM)


CV_TT = 256
CV_HALO = 32
CV_RB = 64


def _conv_kernel(x_ref, wglu_ref, glub_ref, dww_ref, dwb_ref, lng_ref, lnb_ref, o_ref,
                 hbuf_ref, shifted_ref, cbuf_ref):
    tt = x_ref.shape[0]

    @pl.when(pl.program_id(1) == 0)
    def _():
        hbuf_ref[0:CV_HALO, :] = jnp.zeros((CV_HALO, CV_CHANNELS), F32)

    u = _dot(x_ref[...].astype(BF16), wglu_ref[...]) + glub_ref[...]
    hbuf_ref[CV_HALO:CV_HALO + tt, :] = u[:, :CV_CHANNELS] * _sigmoid(u[:, CV_CHANNELS:])

    span = CV_HALO + tt - SUBLANES
    for s in range(1, SUBLANES):
        shifted_ref[s - 1] = hbuf_ref[s:s + span, :]

    base = CV_HALO - (CV_KERNEL - 1)
    for rb in range(tt // CV_RB):
        for s in range(CV_CHANNELS // LANES):
            sl = slice(s * LANES, (s + 1) * LANES)
            acc = jnp.zeros((CV_RB, LANES), F32)
            for j in range(CV_KERNEL):
                shift = (base + j) % SUBLANES
                r0 = rb * CV_RB + base + j - shift
                src = hbuf_ref if shift == 0 else shifted_ref.at[shift - 1]
                acc = acc + dww_ref[j:j + 1, sl] * src[r0:r0 + CV_RB, sl]
            cbuf_ref[rb * CV_RB:(rb + 1) * CV_RB, sl] = acc
    tail = hbuf_ref[tt:tt + CV_HALO, :]
    hbuf_ref[0:CV_HALO, :] = tail

    hc = _layer_norm(cbuf_ref[...] + dwb_ref[...], lng_ref[...], lnb_ref[...])
    o_ref[...] = _silu(hc).astype(o_ref.dtype)


def _conv_branch(x, batch, seq, consts):
    n = x.shape[0]
    tt = min(CV_TT, seq)
    nt = seq // tt
    row = lambda w: pl.BlockSpec((tt, w), lambda b, t: (b * nt + t, 0))
    return pl.pallas_call(
        _conv_kernel,
        grid=(batch, nt),
        in_specs=[row(D_MODEL)] + [spec for _, spec in consts],
        out_specs=row(CV_CHANNELS),
        out_shape=jax.ShapeDtypeStruct((n, CV_CHANNELS), BF16),
        scratch_shapes=[pltpu.VMEM((CV_HALO + tt, CV_CHANNELS), F32),
                        pltpu.VMEM((SUBLANES - 1, CV_HALO + tt - SUBLANES, CV_CHANNELS), F32),
                        pltpu.VMEM((tt, CV_CHANNELS), F32)],
        compiler_params=_params(("parallel", "arbitrary")),
        name="conv_branch",
    )(x, *[a for a, _ in consts])


ROPE_TM = 512
MLA_TM = 512
ATT_TQ = 256
ATT_NB = 2
MLA_Q_SCALE = (MLA_NOPE + MLA_ROPE) ** -0.5 * math.log2(math.e)


def _rope_table_kernel(pos_ref, freq_ref, c_ref, s1_ref, s2_ref):
    half = MLA_ROPE // 2
    ang = pos_ref[...] * freq_ref[...]
    lane = lax.broadcasted_iota(jnp.int32, ang.shape, 1)
    cos = jnp.cos(ang)
    sin = jnp.sin(ang)
    c_ref[...] = jnp.where(lane < MLA_ROPE, cos, 0.0)
    s1_ref[...] = jnp.where((lane >= half) & (lane < MLA_ROPE), sin, 0.0)
    s2_ref[...] = jnp.where(lane < half, -sin, 0.0)


def _rope_tables(pos, freq):
    n = pos.shape[0]
    tm = min(ROPE_TM, n)
    out = pl.BlockSpec((tm, LANES), lambda i: (i, 0))
    return pl.pallas_call(
        _rope_table_kernel,
        grid=(n // tm,),
        in_specs=[pl.BlockSpec((tm, 1), lambda i: (i, 0)), _resident(freq)[1]],
        out_specs=[out, out, out],
        out_shape=[jax.ShapeDtypeStruct((n, LANES), F32)] * 3,
        compiler_params=_params(("parallel",)),
        name="rope_tables",
    )(pos, freq)


def _rope(x, c, s1, s2):
    half = MLA_ROPE // 2
    return x * c + pltpu.roll(x, half, 1) * s1 + pltpu.roll(x, LANES - half, 1) * s2


def _mla_prep_kernel(x_ref, wlat_ref, qnw_ref, kvnw_ref, wuq_ref, wukv_ref,
                     c_ref, s1_ref, s2_ref, q_out, k_out, v_out):
    lat = _dot(x_ref[...].astype(BF16), wlat_ref[...])
    cq = lat[:, :MLA_Q_LORA]
    ckv = lat[:, MLA_Q_LORA:MLA_Q_LORA + MLA_KV_LORA]
    kr = lat[:, MLA_Q_LORA + MLA_KV_LORA:]
    cqn = cq * lax.rsqrt(jnp.mean(cq * cq, axis=-1, keepdims=True) + NORM_EPS) * qnw_ref[...]
    ckvn = ckv * lax.rsqrt(jnp.mean(ckv * ckv, axis=-1, keepdims=True) + NORM_EPS) * kvnw_ref[...]
    q = _dot(cqn.astype(BF16), wuq_ref[...]) * MLA_Q_SCALE
    kv = _dot(ckvn.astype(BF16), wukv_ref[...])
    c, s1, s2 = c_ref[...], s1_ref[...], s2_ref[...]
    kr_rot = _rope(kr, c, s1, s2).astype(BF16)
    for h in range(MLA_HEADS):
        b0 = h * MLA_QK_PAD
        q_out[:, b0:b0 + LANES] = q[:, b0:b0 + LANES].astype(BF16)
        q_out[:, b0 + LANES:b0 + 2 * LANES] = _rope(q[:, b0 + LANES:b0 + 2 * LANES], c, s1, s2).astype(BF16)
        k_out[:, b0:b0 + LANES] = kv[:, b0:b0 + LANES].astype(BF16)
        k_out[:, b0 + LANES:b0 + 2 * LANES] = kr_rot
        v_out[:, h * MLA_V:(h + 1) * MLA_V] = kv[:, b0 + LANES:b0 + 2 * LANES].astype(BF16)


def _mla_prep(x, consts, c, s1, s2):
    n = x.shape[0]
    tm = min(MLA_TM, n)
    row = lambda w: pl.BlockSpec((tm, w), lambda i: (i, 0))
    return pl.pallas_call(
        _mla_prep_kernel,
        grid=(n // tm,),
        in_specs=[row(D_MODEL)] + [spec for _, spec in consts] + [row(LANES)] * 3,
        out_specs=[row(MLA_HEADS * MLA_QK_PAD), row(MLA_HEADS * MLA_QK_PAD), row(MLA_HEADS * MLA_V)],
        out_shape=[jax.ShapeDtypeStruct((n, MLA_HEADS * MLA_QK_PAD), BF16),
                   jax.ShapeDtypeStruct((n, MLA_HEADS * MLA_QK_PAD), BF16),
                   jax.ShapeDtypeStruct((n, MLA_HEADS * MLA_V), BF16)],
        compiler_params=_params(("parallel",)),
        name="mla_prep",
    )(x, *[a for a, _ in consts], c, s1, s2)


def _attention_kernel(q_ref, k_ref, v_ref, o_ref, m_ref, l_ref, acc_ref):
    nb, tq = q_ref.shape[0], q_ref.shape[1]
    i = pl.program_id(1)
    m_ref[...] = jnp.full(m_ref.shape, -1e30, F32)
    l_ref[...] = jnp.zeros(l_ref.shape, F32)
    acc_ref[...] = jnp.zeros(acc_ref.shape, F32)
    r = lax.broadcasted_iota(jnp.int32, (tq, tq), 0)
    c = lax.broadcasted_iota(jnp.int32, (tq, tq), 1)
    pairs = [(b, h) for b in range(nb) for h in range(MLA_HEADS)]
    units = range(len(pairs))
    qk_cols = [slice(h * MLA_QK_PAD, (h + 1) * MLA_QK_PAD) for _, h in pairs]
    v_cols = [slice(h * MLA_V, (h + 1) * MLA_V) for _, h in pairs]

    def kv_block(j, masked):
        rows = pl.ds(pl.multiple_of(j * tq, tq), tq)
        s = [_dot_nt(q_ref[b, :, qk_cols[u]], k_ref[b, rows, qk_cols[u]]) for u, (b, _) in enumerate(pairs)]
        if masked:
            s = [jnp.where(c <= r, s[u], -1e30) for u in units]
        m_prev = [m_ref[u] for u in units]
        m_new = [jnp.maximum(m_prev[u], jnp.max(s[u], axis=-1, keepdims=True)) for u in units]
        p = [jnp.exp2(s[u] - jnp.concatenate([m_new[u]] * (tq // LANES), axis=1)) for u in units]
        alpha = [jnp.exp2(m_prev[u] - m_new[u]) for u in units]
        pv = [_dot(p[u].astype(BF16), v_ref[b, rows, v_cols[u]]) for u, (b, _) in enumerate(pairs)]
        for u in units:
            p_lanes = sum(p[u][:, t * LANES:(t + 1) * LANES] for t in range(1, tq // LANES)) + p[u][:, :LANES]
            l_ref[u] = alpha[u] * l_ref[u] + p_lanes
            acc_ref[u] = alpha[u] * acc_ref[u] + pv[u]
            m_ref[u] = m_new[u]

    def kv_step(j, carry):
        kv_block(j, False)
        return carry

    lax.fori_loop(0, i, kv_step, 0)
    kv_block(i, True)
    for u, (b, _) in enumerate(pairs):
        l = jnp.sum(l_ref[u], axis=-1, keepdims=True)
        o_ref[b, :, v_cols[u]] = (acc_ref[u] / l).astype(o_ref.dtype)


def _attention(q, k, v, batch, seq):
    tq = min(ATT_TQ, seq)
    nb = ATT_NB if batch % ATT_NB == 0 else 1
    qk_w, v_w = MLA_HEADS * MLA_QK_PAD, MLA_HEADS * MLA_V
    stat = pltpu.VMEM((nb * MLA_HEADS, tq, LANES), F32)
    out = pl.pallas_call(
        _attention_kernel,
        grid=(batch // nb, seq // tq),
        in_specs=[pl.BlockSpec((nb, tq, qk_w), lambda b, i: (b, i, 0)),
                  pl.BlockSpec((nb, seq, qk_w), lambda b, i: (b, 0, 0)),
                  pl.BlockSpec((nb, seq, v_w), lambda b, i: (b, 0, 0))],
        out_specs=pl.BlockSpec((nb, tq, v_w), lambda b, i: (b, i, 0)),
        out_shape=jax.ShapeDtypeStruct((batch, seq, v_w), BF16),
        scratch_shapes=[stat, stat, pltpu.VMEM((nb * MLA_HEADS, tq, MLA_V), F32)],
        compiler_params=_params(("parallel", "parallel")),
        name="mla_attention",
    )(q.reshape(batch, seq, qk_w), k.reshape(batch, seq, qk_w), v.reshape(batch, seq, v_w))
    return out.reshape(batch * seq, v_w)


MERGE_TM = 512
MERGE_FC = 256


def _merge_kernel(x_ref, odn_ref, hcv_ref, omla_ref, wg_dn_ref, wg_cv_ref, wg_mla_ref, bgate_ref,
                  wdn_ref, wcv_ref, bcv_ref, wmla_ref, wout_ref, g_ref, b_ref, o_ref, merged_ref):
    x = x_ref[...]
    xb = x.astype(BF16)
    branches = ((odn_ref, wdn_ref, None, wg_dn_ref), (hcv_ref, wcv_ref, bcv_ref, wg_cv_ref),
                (omla_ref, wmla_ref, None, wg_mla_ref))
    for c in range(D_MODEL // MERGE_FC):
        sl = slice(c * MERGE_FC, (c + 1) * MERGE_FC)
        merged = None
        for i, (a_ref, w_ref, bias_ref, wgate_ref) in enumerate(branches):
            gsl = slice(i * D_MODEL + c * MERGE_FC, i * D_MODEL + (c + 1) * MERGE_FC)
            gate = _sigmoid(_dot(xb, wgate_ref[:, sl]) + bgate_ref[:, gsl])
            y = _dot(a_ref[...], w_ref[:, sl])
            if bias_ref is not None:
                y = y + bias_ref[:, sl]
            merged = gate * y if merged is None else merged + gate * y
        merged_ref[:, sl] = merged.astype(BF16)
    y = DEEPNORM_ALPHA * x + _dot(merged_ref[...], wout_ref[...])
    o_ref[...] = _layer_norm(y, g_ref[...], b_ref[...])


def _merge(x, odn, hcv, omla, consts):
    n = x.shape[0]
    tm = min(MERGE_TM, n)
    row = lambda w: pl.BlockSpec((tm, w), lambda i: (i, 0))
    return pl.pallas_call(
        _merge_kernel,
        grid=(n // tm,),
        in_specs=[row(D_MODEL), row(DN_DIM), row(CV_CHANNELS), row(MLA_HEADS * MLA_V)]
                 + [spec for _, spec in consts],
        out_specs=row(D_MODEL),
        out_shape=jax.ShapeDtypeStruct((n, D_MODEL), F32),
        scratch_shapes=[pltpu.VMEM((tm, D_MODEL), BF16)],
        compiler_params=_params(("parallel",)),
        name="merge",
    )(x, odn, hcv, omla, *[a for a, _ in consts])


_IN_SIZES = (DN_DIM, DN_DIM, DN_DIM, DN_DIM, DN_HEADS, DN_HEADS, 2 * CV_CHANNELS,
             MLA_Q_LORA, MLA_KV_LORA, MLA_ROPE, N_BRANCHES * D_MODEL)


_IN_OFFS = tuple(int(v) for v in np.concatenate([[0], np.cumsum(_IN_SIZES)]))
_DN_COLS = _IN_OFFS[4] + LANES


def _pad_lanes(v, width):
    return jnp.pad(v.astype(F32), ((0, 0), (0, width - v.shape[1])))


def _split_w_in(w_in):
    part = lambda i, j=None: w_in[:, :, _IN_OFFS[i]:_IN_OFFS[(i if j is None else j) + 1]]
    pad = jnp.zeros(w_in.shape[:2] + (LANES - MLA_ROPE,), w_in.dtype)
    return dict(
        dn=w_in[:, :, :_DN_COLS].astype(BF16),
        latent=jnp.concatenate([part(7, 9), pad], axis=2).astype(BF16),
        wide=jnp.concatenate([part(6), part(10)], axis=2).astype(BF16),
    )


def _pad_w_uq(w_uq):
    depth = w_uq.shape[0]
    w = w_uq.reshape(depth, MLA_Q_LORA, MLA_HEADS, MLA_NOPE + MLA_ROPE)
    w = jnp.pad(w, ((0, 0), (0, 0), (0, 0), (0, MLA_QK_PAD - MLA_NOPE - MLA_ROPE)))
    return w.reshape(depth, MLA_Q_LORA, MLA_HEADS * MLA_QK_PAD).astype(BF16)


def kernel(x, positions, ln1_g, ln1_b, ffn1_w_gate, ffn1_w_up, ffn1_w_down, w_in, b_gate, dn_conv_w, dn_a_log, dn_dt_bias, dn_norm_w, dn_w_o, cv_glu_b, cv_dw_w, cv_dw_b, cv_ln_g, cv_ln_b, cv_w_pw2, cv_b_pw2, mla_q_norm_w, mla_w_uq, mla_kv_norm_w, mla_w_ukv, mla_w_o, w_out, ln2_g, ln2_b, ffn2_w_gate, ffn2_w_up, ffn2_w_down, ln3_g, ln3_b):
    batch, seq, d = x.shape
    n = batch * seq
    h = x.reshape(n, d)

    half = MLA_ROPE // 2
    inv_freq = ROPE_THETA ** (-np.arange(half, dtype=np.float32) / half)
    freq = np.zeros((1, LANES), np.float32)
    freq[0, :half] = inv_freq
    freq[0, half:MLA_ROPE] = inv_freq
    rope_c, rope_s1, rope_s2 = _rope_tables(positions.astype(F32).reshape(n, 1), jnp.asarray(freq))

    bf = lambda a: a.astype(BF16)
    w = _split_w_in(w_in)
    ffn1 = (bf(ffn1_w_gate), bf(ffn1_w_up), bf(ffn1_w_down))
    ffn2 = (bf(ffn2_w_gate), bf(ffn2_w_up), bf(ffn2_w_down))
    dn_w_o_b, cv_w_pw2_b, mla_w_o_b, w_out_b = bf(dn_w_o), bf(cv_w_pw2), bf(mla_w_o), bf(w_out)
    w_uq_b, w_ukv_b = _pad_w_uq(mla_w_uq), bf(mla_w_ukv)
    a_log_p, dt_bias_p = _pad_lanes(dn_a_log, LANES), _pad_lanes(dn_dt_bias, LANES)
    dn_conv_w, cv_dw_w = dn_conv_w.astype(F32), cv_dw_w.astype(F32)

    for l in range(DEPTH):
        h = _ffn_ln(h, [_layer(a, l) for a in ffn1] + [_vec(ln1_g, l), _vec(ln1_b, l)])
        o_dn = _deltanet(h, batch, seq, [
            _layer(w["dn"], l, 3 * DN_DIM, 0), _layer(w["dn"], l, DN_DIM, 3),
            _layer(w["dn"], l, LANES, _IN_OFFS[4] // LANES), _layer(dn_conv_w, l),
            _vec(a_log_p, l), _vec(dt_bias_p, l), _vec(dn_norm_w, l)])
        h_cv = _conv_branch(h, batch, seq, [
            _layer(w["wide"], l, 2 * CV_CHANNELS, 0), _vec(cv_glu_b, l), _layer(cv_dw_w, l),
            _vec(cv_dw_b, l), _vec(cv_ln_g, l), _vec(cv_ln_b, l)])
        q, k, v = _mla_prep(h, [
            _layer(w["latent"], l), _vec(mla_q_norm_w, l), _vec(mla_kv_norm_w, l),
            _layer(w_uq_b, l), _layer(w_ukv_b, l)], rope_c, rope_s1, rope_s2)
        o_mla = _attention(q, k, v, batch, seq)
        h = _merge(h, o_dn, h_cv, o_mla, [
            _layer(w["wide"], l, D_MODEL, 1), _layer(w["wide"], l, D_MODEL, 2), _layer(w["wide"], l, D_MODEL, 3),
            _vec(b_gate, l), _layer(dn_w_o_b, l), _layer(cv_w_pw2_b, l), _vec(cv_b_pw2, l),
            _layer(mla_w_o_b, l), _layer(w_out_b, l), _vec(ln2_g, l), _vec(ln2_b, l)])
        h = _ffn_ln(h, [_layer(a, l) for a in ffn2] + [_vec(ln3_g, l), _vec(ln3_b, l)])
    return h.reshape(batch, seq, d)
```

```python
import math

import jax
import jax.numpy as jnp
import numpy as np
from jax import lax
from jax.experimental import pallas as pl
from jax.experimental.pallas import tpu as pltpu

D_MODEL = 1024
DEPTH = 2
D_FF = 2816
DN_HEADS = 4
DN_HEAD_DIM = 128
DN_DIM = DN_HEADS * DN_HEAD_DIM
DN_CONV = 4
CV_CHANNELS = 512
CV_KERNEL = 31
MLA_HEADS = 4
MLA_Q_LORA = 384
MLA_KV_LORA = 256
MLA_NOPE = 128
MLA_ROPE = 64
MLA_V = 128
ROPE_THETA = 10000.0
N_BRANCHES = 3
DEEPNORM_ALPHA = (2 * DEPTH) ** 0.25
NORM_EPS = 1e-5

LANES = 128
SUBLANES = 8
MLA_QK_PAD = 256
VMEM_LIMIT = 56 * 1024 * 1024

F32 = jnp.float32
BF16 = jnp.bfloat16
HIGHEST = lax.Precision.HIGHEST


def _dot(a, b, precision=None):
    return jnp.dot(a, b, preferred_element_type=F32, precision=precision)


def _dot_nt(a, b, precision=None):
    return lax.dot_general(a, b, (((1,), (1,)), ((), ())), preferred_element_type=F32, precision=precision)


def _dot_tn(a, b):
    return lax.dot_general(a, b, (((0,), (0,)), ((), ())), preferred_element_type=F32)


def _sigmoid(x):
    return 1.0 / (1.0 + jnp.exp(-x))


def _silu(x):
    return x * _sigmoid(x)


def _layer_norm(y, g, b):
    mu = jnp.mean(y, axis=-1, keepdims=True)
    d = y - mu
    var = jnp.mean(d * d, axis=-1, keepdims=True)
    return d * lax.rsqrt(var + NORM_EPS) * g + b


def _resident(arr):
    return arr, pl.BlockSpec(arr.shape, lambda *_: (0, 0), pipeline_mode=pl.Buffered(1))


def _layer(arr, l, width=None, col=0):
    width = arr.shape[2] if width is None else width
    return arr, pl.BlockSpec((None, arr.shape[1], width), lambda *_: (l, 0, col), pipeline_mode=pl.Buffered(1))


def _vec(arr, l):
    return _layer(arr.reshape(arr.shape[0], 1, arr.shape[1]), l)


def _params(sem):
    return pltpu.CompilerParams(dimension_semantics=sem, vmem_limit_bytes=VMEM_LIMIT)


FFN_TM = 1024
FFN_FC = 256
FFN_RB = 256


def _ffn_ln_kernel(x_ref, wg_ref, wu_ref, wd_ref, g_ref, b_ref, o_ref, h_ref):
    x = x_ref[...]
    xb = x.astype(BF16)
    for c in range(D_FF // FFN_FC):
        sl = slice(c * FFN_FC, (c + 1) * FFN_FC)
        gate = _dot(xb, wg_ref[:, sl])
        up = _dot(xb, wu_ref[:, sl])
        h_ref[:, sl] = (_silu(gate) * up).astype(BF16)
    for r0 in range(0, x.shape[0], FFN_RB):
        rows = slice(r0, min(r0 + FFN_RB, x.shape[0]))
        y = DEEPNORM_ALPHA * x[rows] + 0.5 * _dot(h_ref[rows, :], wd_ref[...])
        o_ref[rows, :] = _layer_norm(y, g_ref[...], b_ref[...])


def _ffn_ln(x, consts):
    n = x.shape[0]
    tm = min(FFN_TM, n)
    row = pl.BlockSpec((tm, D_MODEL), lambda i: (i, 0))
    return pl.pallas_call(
        _ffn_ln_kernel,
        grid=(n // tm,),
        in_specs=[row] + [spec for _, spec in consts],
        out_specs=row,
        out_shape=jax.ShapeDtypeStruct((n, D_MODEL), F32),
        scratch_shapes=[pltpu.VMEM((tm, D_FF), BF16)],
        compiler_params=_params(("parallel",)),
        name="ffn_ln",
    )(x, *[a for a, _ in consts])


CV_TT = 256
CV_HALO = 32
CV_RB = 64


def _conv_kernel(x_ref, wglu_ref, glub_ref, dww_ref, dwb_ref, lng_ref, lnb_ref, o_ref,
                 hbuf_ref, shifted_ref, cbuf_ref):
    tt = x_ref.shape[0]

    @pl.when(pl.program_id(1) == 0)
    def _():
        hbuf_ref[0:CV_HALO, :] = jnp.zeros((CV_HALO, CV_CHANNELS), F32)

    u = _dot(x_ref[...].astype(BF16), wglu_ref[...]) + glub_ref[...]
    hbuf_ref[CV_HALO:CV_HALO + tt, :] = u[:, :CV_CHANNELS] * _sigmoid(u[:, CV_CHANNELS:])

    span = CV_HALO + tt - SUBLANES
    for s in range(1, SUBLANES):
        shifted_ref[s - 1] = hbuf_ref[s:s + span, :]

    base = CV_HALO - (CV_KERNEL - 1)
    for rb in range(tt // CV_RB):
        for s in range(CV_CHANNELS // LANES):
            sl = slice(s * LANES, (s + 1) * LANES)
            acc = jnp.zeros((CV_RB, LANES), F32)
            for j in range(CV_KERNEL):
                shift = (base + j) % SUBLANES
                r0 = rb * CV_RB + base + j - shift
                src = hbuf_ref if shift == 0 else shifted_ref.at[shift - 1]
                acc = acc + dww_ref[j:j + 1, sl] * src[r0:r0 + CV_RB, sl]
            cbuf_ref[rb * CV_RB:(rb + 1) * CV_RB, sl] = acc
    tail = hbuf_ref[tt:tt + CV_HALO, :]
    hbuf_ref[0:CV_HALO, :] = tail

    hc = _layer_norm(cbuf_ref[...] + dwb_ref[...], lng_ref[...], lnb_ref[...])
    o_ref[...] = _silu(hc).astype(o_ref.dtype)


def _conv_branch(x, batch, seq, consts):
    n = x.shape[0]
    tt = min(CV_TT, seq)
    nt = seq // tt
    row = lambda w: pl.BlockSpec((tt, w), lambda b, t: (b * nt + t, 0))
    return pl.pallas_call(
        _conv_kernel,
        grid=(batch, nt),
        in_specs=[row(D_MODEL)] + [spec for _, spec in consts],
        out_specs=row(CV_CHANNELS),
        out_shape=jax.ShapeDtypeStruct((n, CV_CHANNELS), BF16),
        scratch_shapes=[pltpu.VMEM((CV_HALO + tt, CV_CHANNELS), F32),
                        pltpu.VMEM((SUBLANES - 1, CV_HALO + tt - SUBLANES, CV_CHANNELS), F32),
                        pltpu.VMEM((tt, CV_CHANNELS), F32)],
        compiler_params=_params(("parallel", "arbitrary")),
        name="conv_branch",
    )(x, *[a for a, _ in consts])


DN_TT = 256
DN_SUB = 64
DN_HALO = SUBLANES
DN_NB = 2


def _deltanet_kernel(x_ref, wqkv_ref, wz_ref, wab_ref, cw_ref, alog_ref, dtb_ref, nw_ref, o_ref,
                     buf_ref, s_ref):
    nb, tt = x_ref.shape[0], x_ref.shape[1]
    seqs = range(nb)

    @pl.when(pl.program_id(1) == 0)
    def _():
        buf_ref[:, 0:DN_HALO, :] = jnp.zeros((nb, DN_HALO, 3 * DN_DIM), F32)
        s_ref[...] = jnp.zeros(s_ref.shape, F32)

    xb = [x_ref[b].astype(BF16) for b in seqs]
    for b in seqs:
        buf_ref[b, DN_HALO:DN_HALO + tt, :] = _dot(xb[b], wqkv_ref[...])
    z = [_dot(xb[b], wz_ref[...]) for b in seqs]
    ab = [_dot(xb[b], wab_ref[...]) for b in seqs]

    qkv = []
    for b in seqs:
        conv = jnp.zeros((tt, 3 * DN_DIM), F32)
        for j in range(DN_CONV):
            off = DN_HALO - (DN_CONV - 1) + j
            conv = conv + cw_ref[j:j + 1, :] * buf_ref[b, off:off + tt, :]
        tail = buf_ref[b, tt:tt + DN_HALO, :]
        buf_ref[b, 0:DN_HALO, :] = tail
        qkv.append(_silu(conv))

    row = lax.broadcasted_iota(jnp.int32, (tt, tt), 0)
    col = lax.broadcasted_iota(jnp.int32, (tt, tt), 1)
    causal = col <= row
    strict = col < row
    same_sub = (row // DN_SUB) == (col // DN_SUB)
    eye = (col == row).astype(F32)
    tril = causal.astype(F32)
    head_lane = lax.broadcasted_iota(jnp.int32, (tt, LANES), 1) < DN_HEADS

    betas, gcs, gcs_t = [], [], []
    for b in seqs:
        sp_in = ab[b] + dtb_ref[...]
        softplus = jnp.maximum(sp_in, 0.0) + jnp.log(1.0 + jnp.exp(-jnp.abs(sp_in)))
        g = jnp.where(head_lane, -jnp.exp(alog_ref[...]) * softplus, 0.0)
        betas.append(_sigmoid(ab[b]))
        gcs.append(_dot(tril, g, HIGHEST))
        gcs_t.append(gcs[b].T)

    pairs = [(b, h) for b in seqs for h in range(DN_HEADS)]
    heads = range(len(pairs))
    hsl = [slice(h * DN_HEAD_DIM, (h + 1) * DN_HEAD_DIM) for _, h in pairs]
    scale = DN_HEAD_DIM ** -0.5

    def unit(x):
        return x * lax.rsqrt(jnp.sum(x * x, axis=-1, keepdims=True) + 1e-6)

    q = [unit(qkv[b][:, h * DN_HEAD_DIM:(h + 1) * DN_HEAD_DIM]) * scale for b, h in pairs]
    k = [unit(qkv[b][:, DN_DIM + h * DN_HEAD_DIM:DN_DIM + (h + 1) * DN_HEAD_DIM]) for b, h in pairs]
    v = [qkv[b][:, 2 * DN_DIM + h * DN_HEAD_DIM:2 * DN_DIM + (h + 1) * DN_HEAD_DIM] for b, h in pairs]
    gc = [gcs[b][:, h:h + 1] for b, h in pairs]
    beta = [betas[b][:, DN_HEADS + h:DN_HEADS + h + 1] for b, h in pairs]
    decay = [jnp.where(causal, jnp.exp(jnp.minimum(gc[u] - gcs_t[b][h:h + 1, :], 0.0)), 0.0)
             for u, (b, h) in enumerate(pairs)]
    kb = [k[h].astype(BF16) for h in heads]
    kk = [_dot_nt(kb[h], kb[h]) for h in heads]
    qk = [_dot_nt(q[h].astype(BF16), kb[h]) for h in heads]
    lmat = [jnp.where(strict, beta[h] * kk[h] * decay[h], 0.0) for h in heads]
    l_diag = [jnp.where(same_sub, lmat[h], 0.0) for h in heads]
    inv_d = [eye - l_diag[h] for h in heads]
    power = [l_diag[h].astype(BF16) for h in heads]
    for _ in range(int(math.log2(DN_SUB)) - 1):
        power = [_dot(power[h], power[h]).astype(BF16) for h in heads]
        inv_d = [inv_d[h] + _dot(inv_d[h].astype(BF16), power[h]) for h in heads]
    inv_db = [inv_d[h].astype(BF16) for h in heads]
    nmat = [_dot(inv_db[h], (lmat[h] - l_diag[h]).astype(BF16)) for h in heads]
    inv_n = [eye - nmat[h] for h in heads]
    power = [nmat[h].astype(BF16) for h in heads]
    for _ in range(int(math.log2(tt // DN_SUB)) - 1):
        power = [_dot(power[h], power[h]).astype(BF16) for h in heads]
        inv_n = [inv_n[h] + _dot(inv_n[h].astype(BF16), power[h]) for h in heads]
    eg = [jnp.exp(gc[h]) for h in heads]
    rhs = [jnp.concatenate([v[h] * beta[h], k[h] * (beta[h] * eg[h])], axis=1).astype(BF16) for h in heads]
    half = [_dot(inv_db[h], rhs[h]).astype(BF16) for h in heads]
    sol = [_dot(inv_n[h].astype(BF16), half[h]) for h in heads]
    qkm = [jnp.where(causal, qk[h] * decay[h], 0.0).astype(BF16) for h in heads]
    g_last = [gc[h][tt - 1:tt, :] for h in heads]
    q_dec = [(q[h] * eg[h]).astype(BF16) for h in heads]
    k_dec = [(k[h] * jnp.exp(g_last[h] - gc[h])).astype(BF16) for h in heads]
    state = [s_ref[h] for h in heads]
    sb = [state[h].astype(BF16) for h in heads]
    vb = [(sol[h][:, :DN_HEAD_DIM] - _dot(sol[h][:, DN_HEAD_DIM:].astype(BF16), sb[h])).astype(BF16)
          for h in heads]
    o = [_dot(q_dec[h], sb[h]) + _dot(qkm[h], vb[h]) for h in heads]
    for h in heads:
        s_ref[h] = state[h] * jnp.exp(g_last[h]) + _dot_tn(k_dec[h], vb[h])
    for u, (b, _) in enumerate(pairs):
        on = o[u] * lax.rsqrt(jnp.mean(o[u] * o[u], axis=-1, keepdims=True) + NORM_EPS) * nw_ref[...]
        o_ref[b, :, hsl[u]] = (on * _silu(z[b][:, hsl[u]])).astype(o_ref.dtype)


def _deltanet(x, batch, seq, consts):
    tt = min(DN_TT, seq)
    nb = DN_NB if batch % DN_NB == 0 else 1
    block = lambda w: pl.BlockSpec((nb, tt, w), lambda i, t: (i, t, 0))
    out = pl.pallas_call(
        _deltanet_kernel,
        grid=(batch // nb, seq // tt),
        in_specs=[block(D_MODEL)] + [spec for _, spec in consts],
        out_specs=block(DN_DIM),
        out_shape=jax.ShapeDtypeStruct((batch, seq, DN_DIM), BF16),
        scratch_shapes=[
            pltpu.VMEM((nb, DN_HALO + tt, 3 * DN_DIM), F32),
            pltpu.VMEM((nb * DN_HEADS, DN_HEAD_DIM, DN_HEAD_DIM), F32),
        ],
        compiler_params=_params(("parallel", "arbitrary")),
        name="deltanet",
    )(x.reshape(batch, seq, D_MODEL), *[a for a, _ in consts])
    return out.reshape(batch * seq, DN_DIM)


ROPE_TM = 512
MLA_TM = 512
ATT_TQ = 256
ATT_NB = 2
MLA_Q_SCALE = (MLA_NOPE + MLA_ROPE) ** -0.5 * math.log2(math.e)


def _rope_table_kernel(pos_ref, freq_ref, c_ref, s1_ref, s2_ref):
    half = MLA_ROPE // 2
    ang = pos_ref[...] * freq_ref[...]
    lane = lax.broadcasted_iota(jnp.int32, ang.shape, 1)
    cos = jnp.cos(ang)
    sin = jnp.sin(ang)
    c_ref[...] = jnp.where(lane < MLA_ROPE, cos, 0.0)
    s1_ref[...] = jnp.where((lane >= half) & (lane < MLA_ROPE), sin, 0.0)
    s2_ref[...] = jnp.where(lane < half, -sin, 0.0)


def _rope_tables(pos, freq):
    n = pos.shape[0]
    tm = min(ROPE_TM, n)
    out = pl.BlockSpec((tm, LANES), lambda i: (i, 0))
    return pl.pallas_call(
        _rope_table_kernel,
        grid=(n // tm,),
        in_specs=[pl.BlockSpec((tm, 1), lambda i: (i, 0)), _resident(freq)[1]],
        out_specs=[out, out, out],
        out_shape=[jax.ShapeDtypeStruct((n, LANES), F32)] * 3,
        compiler_params=_params(("parallel",)),
        name="rope_tables",
    )(pos, freq)


def _rope(x, c, s1, s2):
    half = MLA_ROPE // 2
    return x * c + pltpu.roll(x, half, 1) * s1 + pltpu.roll(x, LANES - half, 1) * s2


def _mla_prep_kernel(x_ref, wlat_ref, qnw_ref, kvnw_ref, wuq_ref, wukv_ref,
                     c_ref, s1_ref, s2_ref, q_out, k_out, v_out):
    lat = _dot(x_ref[...].astype(BF16), wlat_ref[...])
    cq = lat[:, :MLA_Q_LORA]
    ckv = lat[:, MLA_Q_LORA:MLA_Q_LORA + MLA_KV_LORA]
    kr = lat[:, MLA_Q_LORA + MLA_KV_LORA:]
    cqn = cq * lax.rsqrt(jnp.mean(cq * cq, axis=-1, keepdims=True) + NORM_EPS) * qnw_ref[...]
    ckvn = ckv * lax.rsqrt(jnp.mean(ckv * ckv, axis=-1, keepdims=True) + NORM_EPS) * kvnw_ref[...]
    q = _dot(cqn.astype(BF16), wuq_ref[...]) * MLA_Q_SCALE
    kv = _dot(ckvn.astype(BF16), wukv_ref[...])
    c, s1, s2 = c_ref[...], s1_ref[...], s2_ref[...]
    kr_rot = _rope(kr, c, s1, s2).astype(BF16)
    for h in range(MLA_HEADS):
        b0 = h * MLA_QK_PAD
        q_out[:, b0:b0 + LANES] = q[:, b0:b0 + LANES].astype(BF16)
        q_out[:, b0 + LANES:b0 + 2 * LANES] = _rope(q[:, b0 + LANES:b0 + 2 * LANES], c, s1, s2).astype(BF16)
        k_out[:, b0:b0 + LANES] = kv[:, b0:b0 + LANES].astype(BF16)
        k_out[:, b0 + LANES:b0 + 2 * LANES] = kr_rot
        v_out[:, h * MLA_V:(h + 1) * MLA_V] = kv[:, b0 + LANES:b0 + 2 * LANES].astype(BF16)


def _mla_prep(x, consts, c, s1, s2):
    n = x.shape[0]
    tm = min(MLA_TM, n)
    row = lambda w: pl.BlockSpec((tm, w), lambda i: (i, 0))
    return pl.pallas_call(
        _mla_prep_kernel,
        grid=(n // tm,),
        in_specs=[row(D_MODEL)] + [spec for _, spec in consts] + [row(LANES)] * 3,
        out_specs=[row(MLA_HEADS * MLA_QK_PAD), row(MLA_HEADS * MLA_QK_PAD), row(MLA_HEADS * MLA_V)],
        out_shape=[jax.ShapeDtypeStruct((n, MLA_HEADS * MLA_QK_PAD), BF16),
                   jax.ShapeDtypeStruct((n, MLA_HEADS * MLA_QK_PAD), BF16),
                   jax.ShapeDtypeStruct((n, MLA_HEADS * MLA_V), BF16)],
        compiler_params=_params(("parallel",)),
        name="mla_prep",
    )(x, *[a for a, _ in consts], c, s1, s2)


def _attention_kernel(q_ref, k_ref, v_ref, o_ref, m_ref, l_ref, acc_ref):
    nb, tq = q_ref.shape[0], q_ref.shape[1]
    i = pl.program_id(1)
    m_ref[...] = jnp.full(m_ref.shape, -1e30, F32)
    l_ref[...] = jnp.zeros(l_ref.shape, F32)
    acc_ref[...] = jnp.zeros(acc_ref.shape, F32)
    r = lax.broadcasted_iota(jnp.int32, (tq, tq), 0)
    c = lax.broadcasted_iota(jnp.int32, (tq, tq), 1)
    pairs = [(b, h) for b in range(nb) for h in range(MLA_HEADS)]
    units = range(len(pairs))
    qk_cols = [slice(h * MLA_QK_PAD, (h + 1) * MLA_QK_PAD) for _, h in pairs]
    v_cols = [slice(h * MLA_V, (h + 1) * MLA_V) for _, h in pairs]

    def kv_block(j, masked):
        rows = pl.ds(pl.multiple_of(j * tq, tq), tq)
        s = [_dot_nt(q_ref[b, :, qk_cols[u]], k_ref[b, rows, qk_cols[u]]) for u, (b, _) in enumerate(pairs)]
        if masked:
            s = [jnp.where(c <= r, s[u], -1e30) for u in units]
        m_prev = [m_ref[u] for u in units]
        m_new = [jnp.maximum(m_prev[u], jnp.max(s[u], axis=-1, keepdims=True)) for u in units]
        p = [jnp.exp2(s[u] - jnp.concatenate([m_new[u]] * (tq // LANES), axis=1)) for u in units]
        alpha = [jnp.exp2(m_prev[u] - m_new[u]) for u in units]
        pv = [_dot(p[u].astype(BF16), v_ref[b, rows, v_cols[u]]) for u, (b, _) in enumerate(pairs)]
        for u in units:
            p_lanes = sum(p[u][:, t * LANES:(t + 1) * LANES] for t in range(1, tq // LANES)) + p[u][:, :LANES]
            l_ref[u] = alpha[u] * l_ref[u] + p_lanes
            acc_ref[u] = alpha[u] * acc_ref[u] + pv[u]
            m_ref[u] = m_new[u]

    def kv_step(j, carry):
        kv_block(j, False)
        return carry

    lax.fori_loop(0, i, kv_step, 0)
    kv_block(i, True)
    for u, (b, _) in enumerate(pairs):
        l = jnp.sum(l_ref[u], axis=-1, keepdims=True)
        o_ref[b, :, v_cols[u]] = (acc_ref[u] / l).astype(o_ref.dtype)


def _attention(q, k, v, batch, seq):
    tq = min(ATT_TQ, seq)
    nb = ATT_NB if batch % ATT_NB == 0 else 1
    qk_w, v_w = MLA_HEADS * MLA_QK_PAD, MLA_HEADS * MLA_V
    stat = pltpu.VMEM((nb * MLA_HEADS, tq, LANES), F32)
    out = pl.pallas_call(
        _attention_kernel,
        grid=(batch // nb, seq // tq),
        in_specs=[pl.BlockSpec((nb, tq, qk_w), lambda b, i: (b, i, 0)),
                  pl.BlockSpec((nb, seq, qk_w), lambda b, i: (b, 0, 0)),
                  pl.BlockSpec((nb, seq, v_w), lambda b, i: (b, 0, 0))],
        out_specs=pl.BlockSpec((nb, tq, v_w), lambda b, i: (b, i, 0)),
        out_shape=jax.ShapeDtypeStruct((batch, seq, v_w), BF16),
        scratch_shapes=[stat, stat, pltpu.VMEM((nb * MLA_HEADS, tq, MLA_V), F32)],
        compiler_params=_params(("parallel", "parallel")),
        name="mla_attention",
    )(q.reshape(batch, seq, qk_w), k.reshape(batch, seq, qk_w), v.reshape(batch, seq, v_w))
    return out.reshape(batch * seq, v_w)


MERGE_TM = 512
MERGE_FC = 256


def _merge_kernel(x_ref, odn_ref, hcv_ref, omla_ref, wg_dn_ref, wg_cv_ref, wg_mla_ref, bgate_ref,
                  wdn_ref, wcv_ref, bcv_ref, wmla_ref, wout_ref, g_ref, b_ref, o_ref, merged_ref):
    x = x_ref[...]
    xb = x.astype(BF16)
    branches = ((odn_ref, wdn_ref, None, wg_dn_ref), (hcv_ref, wcv_ref, bcv_ref, wg_cv_ref),
                (omla_ref, wmla_ref, None, wg_mla_ref))
    for c in range(D_MODEL // MERGE_FC):
        sl = slice(c * MERGE_FC, (c + 1) * MERGE_FC)
        merged = None
        for i, (a_ref, w_ref, bias_ref, wgate_ref) in enumerate(branches):
            gsl = slice(i * D_MODEL + c * MERGE_FC, i * D_MODEL + (c + 1) * MERGE_FC)
            gate = _sigmoid(_dot(xb, wgate_ref[:, sl]) + bgate_ref[:, gsl])
            y = _dot(a_ref[...], w_ref[:, sl])
            if bias_ref is not None:
                y = y + bias_ref[:, sl]
            merged = gate * y if merged is None else merged + gate * y
        merged_ref[:, sl] = merged.astype(BF16)
    for r0 in range(0, x.shape[0], FFN_RB):
        rows = slice(r0, min(r0 + FFN_RB, x.shape[0]))
        y = DEEPNORM_ALPHA * x[rows] + _dot(merged_ref[rows, :], wout_ref[...])
        o_ref[rows, :] = _layer_norm(y, g_ref[...], b_ref[...])


def _merge(x, odn, hcv, omla, consts):
    n = x.shape[0]
    tm = min(MERGE_TM, n)
    row = lambda w: pl.BlockSpec((tm, w), lambda i: (i, 0))
    return pl.pallas_call(
        _merge_kernel,
        grid=(n // tm,),
        in_specs=[row(D_MODEL), row(DN_DIM), row(CV_CHANNELS), row(MLA_HEADS * MLA_V)]
                 + [spec for _, spec in consts],
        out_specs=row(D_MODEL),
        out_shape=jax.ShapeDtypeStruct((n, D_MODEL), F32),
        scratch_shapes=[pltpu.VMEM((tm, D_MODEL), BF16)],
        compiler_params=_params(("parallel",)),
        name="merge",
    )(x, odn, hcv, omla, *[a for a, _ in consts])


_IN_SIZES = (DN_DIM, DN_DIM, DN_DIM, DN_DIM, DN_HEADS, DN_HEADS, 2 * CV_CHANNELS,
             MLA_Q_LORA, MLA_KV_LORA, MLA_ROPE, N_BRANCHES * D_MODEL)


_IN_OFFS = tuple(int(v) for v in np.concatenate([[0], np.cumsum(_IN_SIZES)]))


def _pad_lanes(v, width):
    return jnp.pad(v.astype(F32), ((0, 0), (0, width - v.shape[1])))


def _split_w_in(w_in):
    w = w_in.astype(BF16)
    part = lambda i, j=None: w[:, :, _IN_OFFS[i]:_IN_OFFS[(i if j is None else j) + 1]]
    pad = jnp.zeros(w.shape[:2] + (LANES - MLA_ROPE,), BF16)
    return dict(
        dn=w,
        latent=jnp.concatenate([part(7, 9), pad], axis=2),
        wide=jnp.concatenate([part(6), part(10)], axis=2),
    )


def _pad_w_uq(w_uq):
    depth = w_uq.shape[0]
    w = w_uq.reshape(depth, MLA_Q_LORA, MLA_HEADS, MLA_NOPE + MLA_ROPE)
    w = jnp.pad(w, ((0, 0), (0, 0), (0, 0), (0, MLA_QK_PAD - MLA_NOPE - MLA_ROPE)))
    return w.reshape(depth, MLA_Q_LORA, MLA_HEADS * MLA_QK_PAD).astype(BF16)


def kernel(x, positions, ln1_g, ln1_b, ffn1_w_gate, ffn1_w_up, ffn1_w_down, w_in, b_gate, dn_conv_w, dn_a_log, dn_dt_bias, dn_norm_w, dn_w_o, cv_glu_b, cv_dw_w, cv_dw_b, cv_ln_g, cv_ln_b, cv_w_pw2, cv_b_pw2, mla_q_norm_w, mla_w_uq, mla_kv_norm_w, mla_w_ukv, mla_w_o, w_out, ln2_g, ln2_b, ffn2_w_gate, ffn2_w_up, ffn2_w_down, ln3_g, ln3_b):
    batch, seq, d = x.shape
    n = batch * seq
    h = x.reshape(n, d)

    half = MLA_ROPE // 2
    inv_freq = ROPE_THETA ** (-np.arange(half, dtype=np.float32) / half)
    freq = np.zeros((1, LANES), np.float32)
    freq[0, :half] = inv_freq
    freq[0, half:MLA_ROPE] = inv_freq
    rope_c, rope_s1, rope_s2 = _rope_tables(positions.astype(F32).reshape(n, 1), jnp.asarray(freq))

    bf = lambda a: a.astype(BF16)
    w = _split_w_in(w_in)
    ffn1 = (bf(ffn1_w_gate), bf(ffn1_w_up), bf(ffn1_w_down))
    ffn2 = (bf(ffn2_w_gate), bf(ffn2_w_up), bf(ffn2_w_down))
    dn_w_o_b, cv_w_pw2_b, mla_w_o_b, w_out_b = bf(dn_w_o), bf(cv_w_pw2), bf(mla_w_o), bf(w_out)
    w_uq_b, w_ukv_b = _pad_w_uq(mla_w_uq), bf(mla_w_ukv)
    a_log_p, dt_bias_p = _pad_lanes(dn_a_log, LANES), _pad_lanes(dn_dt_bias, LANES)
    dn_conv_w, cv_dw_w = dn_conv_w.astype(F32), cv_dw_w.astype(F32)

    for l in range(DEPTH):
        h = _ffn_ln(h, [_layer(a, l) for a in ffn1] + [_vec(ln1_g, l), _vec(ln1_b, l)])
        h_cv = _conv_branch(h, batch, seq, [
            _layer(w["wide"], l, 2 * CV_CHANNELS, 0), _vec(cv_glu_b, l), _layer(cv_dw_w, l),
            _vec(cv_dw_b, l), _vec(cv_ln_g, l), _vec(cv_ln_b, l)])
        o_dn = _deltanet(h, batch, seq, [
            _layer(w["dn"], l, 3 * DN_DIM, 0), _layer(w["dn"], l, DN_DIM, 3),
            _layer(w["dn"], l, LANES, _IN_OFFS[4] // LANES), _layer(dn_conv_w, l),
            _vec(a_log_p, l), _vec(dt_bias_p, l), _vec(dn_norm_w, l)])
        q, k, v = _mla_prep(h, [
            _layer(w["latent"], l), _vec(mla_q_norm_w, l), _vec(mla_kv_norm_w, l),
            _layer(w_uq_b, l), _layer(w_ukv_b, l)], rope_c, rope_s1, rope_s2)
        o_mla = _attention(q, k, v, batch, seq)
        h = _merge(h, o_dn, h_cv, o_mla, [
            _layer(w["wide"], l, D_MODEL, 1), _layer(w["wide"], l, D_MODEL, 2), _layer(w["wide"], l, D_MODEL, 3),
            _vec(b_gate, l), _layer(dn_w_o_b, l), _layer(cv_w_pw2_b, l), _vec(cv_b_pw2, l),
            _layer(mla_w_o_b, l), _layer(w_out_b, l), _vec(ln2_g, l), _vec(ln2_b, l)])
        h = _ffn_ln(h, [_layer(a, l) for a in ffn2] + [_vec(ln3_g, l), _vec(ln3_b, l)])
    return h.reshape(batch, seq, d)
```

```python
import math

import jax
import jax.numpy as jnp
import numpy as np
from jax import lax
from jax.experimental import pallas as pl
from jax.experimental.pallas import tpu as pltpu

D_MODEL = 1024
DEPTH = 2
D_FF = 2816
DN_HEADS = 4
DN_HEAD_DIM = 128
DN_DIM = DN_HEADS * DN_HEAD_DIM
DN_CONV = 4
CV_CHANNELS = 512
CV_KERNEL = 31
MLA_HEADS = 4
MLA_Q_LORA = 384
MLA_KV_LORA = 256
MLA_NOPE = 128
MLA_ROPE = 64
MLA_V = 128
ROPE_THETA = 10000.0
N_BRANCHES = 3
DEEPNORM_ALPHA = (2 * DEPTH) ** 0.25
NORM_EPS = 1e-5

LANES = 128
SUBLANES = 8
MLA_QK_PAD = 256
VMEM_LIMIT = 56 * 1024 * 1024

F32 = jnp.float32
BF16 = jnp.bfloat16
HIGHEST = lax.Precision.HIGHEST


def _dot(a, b, precision=None):
    return jnp.dot(a, b, preferred_element_type=F32, precision=precision)


def _dot_nt(a, b, precision=None):
    return lax.dot_general(a, b, (((1,), (1,)), ((), ())), preferred_element_type=F32, precision=precision)


def _dot_tn(a, b):
    return lax.dot_general(a, b, (((0,), (0,)), ((), ())), preferred_element_type=F32)


def _sigmoid(x):
    return 1.0 / (1.0 + jnp.exp(-x))


def _silu(x):
    return x * _sigmoid(x)


def _layer_norm(y, g, b):
    mu = jnp.mean(y, axis=-1, keepdims=True)
    d = y - mu
    var = jnp.mean(d * d, axis=-1, keepdims=True)
    return d * lax.rsqrt(var + NORM_EPS) * g + b


def _resident(arr):
    return arr, pl.BlockSpec(arr.shape, lambda *_: (0, 0), pipeline_mode=pl.Buffered(1))


def _layer(arr, l, width=None, col=0):
    width = arr.shape[2] if width is None else width
    return arr, pl.BlockSpec((None, arr.shape[1], width), lambda *_: (l, 0, col), pipeline_mode=pl.Buffered(1))


def _vec(arr, l):
    return _layer(arr.reshape(arr.shape[0], 1, arr.shape[1]), l)


def _params(sem):
    return pltpu.CompilerParams(dimension_semantics=sem, vmem_limit_bytes=VMEM_LIMIT)


FFN_TM = 1024
FFN_FC = 256
FFN_RB = 256


def _ffn_ln_kernel(x_ref, wg_ref, wu_ref, wd_ref, g_ref, b_ref, o_ref, h_ref):
    x = x_ref[...]
    xb = x.astype(BF16)
    for c in range(D_FF // FFN_FC):
        sl = slice(c * FFN_FC, (c + 1) * FFN_FC)
        gate = _dot(xb, wg_ref[:, sl])
        up = _dot(xb, wu_ref[:, sl])
        h_ref[:, sl] = (_silu(gate) * up).astype(BF16)
    for r0 in range(0, x.shape[0], FFN_RB):
        rows = slice(r0, min(r0 + FFN_RB, x.shape[0]))
        y = DEEPNORM_ALPHA * x[rows] + 0.5 * _dot(h_ref[rows, :], wd_ref[...])
        o_ref[rows, :] = _layer_norm(y, g_ref[...], b_ref[...])


def _ffn_ln(x, consts):
    n = x.shape[0]
    tm = min(FFN_TM, n)
    row = pl.BlockSpec((tm, D_MODEL), lambda i: (i, 0))
    return pl.pallas_call(
        _ffn_ln_kernel,
        grid=(n // tm,),
        in_specs=[row] + [spec for _, spec in consts],
        out_specs=row,
        out_shape=jax.ShapeDtypeStruct((n, D_MODEL), F32),
        scratch_shapes=[pltpu.VMEM((tm, D_FF), BF16)],
        compiler_params=_params(("parallel",)),
        name="ffn_ln",
    )(x, *[a for a, _ in consts])


CV_TT = 256
CV_HALO = 32
CV_RB = 64


def _conv_kernel(x_ref, wglu_ref, glub_ref, dww_ref, dwb_ref, lng_ref, lnb_ref, o_ref,
                 hbuf_ref, shifted_ref, cbuf_ref):
    tt = x_ref.shape[0]

    @pl.when(pl.program_id(1) == 0)
    def _():
        hbuf_ref[0:CV_HALO, :] = jnp.zeros((CV_HALO, CV_CHANNELS), F32)

    u = _dot(x_ref[...].astype(BF16), wglu_ref[...]) + glub_ref[...]
    hbuf_ref[CV_HALO:CV_HALO + tt, :] = u[:, :CV_CHANNELS] * _sigmoid(u[:, CV_CHANNELS:])

    span = CV_HALO + tt - SUBLANES
    for s in range(1, SUBLANES):
        shifted_ref[s - 1] = hbuf_ref[s:s + span, :]

    base = CV_HALO - (CV_KERNEL - 1)
    for rb in range(tt // CV_RB):
        for s in range(CV_CHANNELS // LANES):
            sl = slice(s * LANES, (s + 1) * LANES)
            acc = jnp.zeros((CV_RB, LANES), F32)
            for j in range(CV_KERNEL):
                shift = (base + j) % SUBLANES
                r0 = rb * CV_RB + base + j - shift
                src = hbuf_ref if shift == 0 else shifted_ref.at[shift - 1]
                acc = acc + dww_ref[j:j + 1, sl] * src[r0:r0 + CV_RB, sl]
            cbuf_ref[rb * CV_RB:(rb + 1) * CV_RB, sl] = acc
    tail = hbuf_ref[tt:tt + CV_HALO, :]
    hbuf_ref[0:CV_HALO, :] = tail

    hc = _layer_norm(cbuf_ref[...] + dwb_ref[...], lng_ref[...], lnb_ref[...])
    o_ref[...] = _silu(hc).astype(o_ref.dtype)


def _conv_branch(x, batch, seq, consts):
    n = x.shape[0]
    tt = min(CV_TT, seq)
    nt = seq // tt
    row = lambda w: pl.BlockSpec((tt, w), lambda b, t: (b * nt + t, 0))
    return pl.pallas_call(
        _conv_kernel,
        grid=(batch, nt),
        in_specs=[row(D_MODEL)] + [spec for _, spec in consts],
        out_specs=row(CV_CHANNELS),
        out_shape=jax.ShapeDtypeStruct((n, CV_CHANNELS), BF16),
        scratch_shapes=[pltpu.VMEM((CV_HALO + tt, CV_CHANNELS), F32),
                        pltpu.VMEM((SUBLANES - 1, CV_HALO + tt - SUBLANES, CV_CHANNELS), F32),
                        pltpu.VMEM((tt, CV_CHANNELS), F32)],
        compiler_params=_params(("parallel", "arbitrary")),
        name="conv_branch",
    )(x, *[a for a, _ in consts])


DN_TT = 256
DN_SUB = 64
DN_HALO = SUBLANES
DN_NB = 2


def _deltanet_kernel(x_ref, wqkv32_ref, wz32_ref, wab32_ref, cw_ref, alog_ref, dtb_ref, nw_ref, o_ref,
                     buf_ref, s_ref, wqkv_ref, wz_ref, wab_ref):
    nb, tt = x_ref.shape[0], x_ref.shape[1]
    seqs = range(nb)

    @pl.when(pl.program_id(1) == 0)
    def _():
        buf_ref[:, 0:DN_HALO, :] = jnp.zeros((nb, DN_HALO, 3 * DN_DIM), F32)
        s_ref[...] = jnp.zeros(s_ref.shape, F32)
        wqkv_ref[...] = wqkv32_ref[...].astype(BF16)
        wz_ref[...] = wz32_ref[...].astype(BF16)
        wab_ref[...] = wab32_ref[...].astype(BF16)

    xb = [x_ref[b].astype(BF16) for b in seqs]
    for b in seqs:
        buf_ref[b, DN_HALO:DN_HALO + tt, :] = _dot(xb[b], wqkv_ref[...])
    z = [_dot(xb[b], wz_ref[...]) for b in seqs]
    ab = [_dot(xb[b], wab_ref[...]) for b in seqs]

    qkv = []
    for b in seqs:
        conv = jnp.zeros((tt, 3 * DN_DIM), F32)
        for j in range(DN_CONV):
            off = DN_HALO - (DN_CONV - 1) + j
            conv = conv + cw_ref[j:j + 1, :] * buf_ref[b, off:off + tt, :]
        tail = buf_ref[b, tt:tt + DN_HALO, :]
        buf_ref[b, 0:DN_HALO, :] = tail
        qkv.append(_silu(conv))

    row = lax.broadcasted_iota(jnp.int32, (tt, tt), 0)
    col = lax.broadcasted_iota(jnp.int32, (tt, tt), 1)
    causal = col <= row
    strict = col < row
    same_sub = (row // DN_SUB) == (col // DN_SUB)
    eye = (col == row).astype(F32)
    tril = causal.astype(F32)
    head_lane = lax.broadcasted_iota(jnp.int32, (tt, LANES), 1) < DN_HEADS

    betas, gcs, gcs_t = [], [], []
    for b in seqs:
        sp_in = ab[b] + dtb_ref[...]
        softplus = jnp.maximum(sp_in, 0.0) + jnp.log(1.0 + jnp.exp(-jnp.abs(sp_in)))
        g = jnp.where(head_lane, -jnp.exp(alog_ref[...]) * softplus, 0.0)
        betas.append(_sigmoid(ab[b]))
        gcs.append(_dot(tril, g, HIGHEST))
        gcs_t.append(gcs[b].T)

    pairs = [(b, h) for b in seqs for h in range(DN_HEADS)]
    heads = range(len(pairs))
    hsl = [slice(h * DN_HEAD_DIM, (h + 1) * DN_HEAD_DIM) for _, h in pairs]
    scale = DN_HEAD_DIM ** -0.5

    def unit(x):
        return x * lax.rsqrt(jnp.sum(x * x, axis=-1, keepdims=True) + 1e-6)

    q = [unit(qkv[b][:, h * DN_HEAD_DIM:(h + 1) * DN_HEAD_DIM]) * scale for b, h in pairs]
    k = [unit(qkv[b][:, DN_DIM + h * DN_HEAD_DIM:DN_DIM + (h + 1) * DN_HEAD_DIM]) for b, h in pairs]
    v = [qkv[b][:, 2 * DN_DIM + h * DN_HEAD_DIM:2 * DN_DIM + (h + 1) * DN_HEAD_DIM] for b, h in pairs]
    gc = [gcs[b][:, h:h + 1] for b, h in pairs]
    beta = [betas[b][:, DN_HEADS + h:DN_HEADS + h + 1] for b, h in pairs]
    decay = [jnp.where(causal, jnp.exp(jnp.minimum(gc[u] - gcs_t[b][h:h + 1, :], 0.0)), 0.0)
             for u, (b, h) in enumerate(pairs)]
    kb = [k[h].astype(BF16) for h in heads]
    kk = [_dot_nt(kb[h], kb[h]) for h in heads]
    qk = [_dot_nt(q[h].astype(BF16), kb[h]) for h in heads]
    lmat = [jnp.where(strict, beta[h] * kk[h] * decay[h], 0.0) for h in heads]
    l_diag = [jnp.where(same_sub, lmat[h], 0.0) for h in heads]
    inv_d = [eye - l_diag[h] for h in heads]
    power = [l_diag[h].astype(BF16) for h in heads]
    for _ in range(int(math.log2(DN_SUB)) - 1):
        power = [_dot(power[h], power[h]).astype(BF16) for h in heads]
        inv_d = [inv_d[h] + _dot(inv_d[h].astype(BF16), power[h]) for h in heads]
    inv_db = [inv_d[h].astype(BF16) for h in heads]
    nmat = [_dot(inv_db[h], (lmat[h] - l_diag[h]).astype(BF16)) for h in heads]
    inv_n = [eye - nmat[h] for h in heads]
    power = [nmat[h].astype(BF16) for h in heads]
    for _ in range(int(math.log2(tt // DN_SUB)) - 1):
        power = [_dot(power[h], power[h]).astype(BF16) for h in heads]
        inv_n = [inv_n[h] + _dot(inv_n[h].astype(BF16), power[h]) for h in heads]
    eg = [jnp.exp(gc[h]) for h in heads]
    rhs = [jnp.concatenate([v[h] * beta[h], k[h] * (beta[h] * eg[h])], axis=1).astype(BF16) for h in heads]
    half = [_dot(inv_db[h], rhs[h]).astype(BF16) for h in heads]
    sol = [_dot(inv_n[h].astype(BF16), half[h]) for h in heads]
    qkm = [jnp.where(causal, qk[h] * decay[h], 0.0).astype(BF16) for h in heads]
    g_last = [gc[h][tt - 1:tt, :] for h in heads]
    q_dec = [(q[h] * eg[h]).astype(BF16) for h in heads]
    k_dec = [(k[h] * jnp.exp(g_last[h] - gc[h])).astype(BF16) for h in heads]
    state = [s_ref[h] for h in heads]
    sb = [state[h].astype(BF16) for h in heads]
    vb = [(sol[h][:, :DN_HEAD_DIM] - _dot(sol[h][:, DN_HEAD_DIM:].astype(BF16), sb[h])).astype(BF16)
          for h in heads]
    o = [_dot(q_dec[h], sb[h]) + _dot(qkm[h], vb[h]) for h in heads]
    for h in heads:
        s_ref[h] = state[h] * jnp.exp(g_last[h]) + _dot_tn(k_dec[h], vb[h])
    for u, (b, _) in enumerate(pairs):
        on = o[u] * lax.rsqrt(jnp.mean(o[u] * o[u], axis=-1, keepdims=True) + NORM_EPS) * nw_ref[...]
        o_ref[b, :, hsl[u]] = (on * _silu(z[b][:, hsl[u]])).astype(o_ref.dtype)


def _deltanet(x, batch, seq, consts):
    tt = min(DN_TT, seq)
    nb = DN_NB if batch % DN_NB == 0 else 1
    block = lambda w: pl.BlockSpec((nb, tt, w), lambda i, t: (i, t, 0))
    out = pl.pallas_call(
        _deltanet_kernel,
        grid=(batch // nb, seq // tt),
        in_specs=[block(D_MODEL)] + [spec for _, spec in consts],
        out_specs=block(DN_DIM),
        out_shape=jax.ShapeDtypeStruct((batch, seq, DN_DIM), BF16),
        scratch_shapes=[
            pltpu.VMEM((nb, DN_HALO + tt, 3 * DN_DIM), F32),
            pltpu.VMEM((nb * DN_HEADS, DN_HEAD_DIM, DN_HEAD_DIM), F32),
            pltpu.VMEM((D_MODEL, 3 * DN_DIM), BF16),
            pltpu.VMEM((D_MODEL, DN_DIM), BF16),
            pltpu.VMEM((D_MODEL, LANES), BF16),
        ],
        compiler_params=_params(("parallel", "arbitrary")),
        name="deltanet",
    )(x.reshape(batch, seq, D_MODEL), *[a for a, _ in consts])
    return out.reshape(batch * seq, DN_DIM)


ROPE_TM = 512
MLA_TM = 512
ATT_TQ = 256
ATT_NB = 2
MLA_Q_SCALE = (MLA_NOPE + MLA_ROPE) ** -0.5 * math.log2(math.e)


def _rope_table_kernel(pos_ref, freq_ref, c_ref, s1_ref, s2_ref):
    half = MLA_ROPE // 2
    ang = pos_ref[...] * freq_ref[...]
    lane = lax.broadcasted_iota(jnp.int32, ang.shape, 1)
    cos = jnp.cos(ang)
    sin = jnp.sin(ang)
    c_ref[...] = jnp.where(lane < MLA_ROPE, cos, 0.0)
    s1_ref[...] = jnp.where((lane >= half) & (lane < MLA_ROPE), sin, 0.0)
    s2_ref[...] = jnp.where(lane < half, -sin, 0.0)


def _rope_tables(pos, freq):
    n = pos.shape[0]
    tm = min(ROPE_TM, n)
    out = pl.BlockSpec((tm, LANES), lambda i: (i, 0))
    return pl.pallas_call(
        _rope_table_kernel,
        grid=(n // tm,),
        in_specs=[pl.BlockSpec((tm, 1), lambda i: (i, 0)), _resident(freq)[1]],
        out_specs=[out, out, out],
        out_shape=[jax.ShapeDtypeStruct((n, LANES), F32)] * 3,
        compiler_params=_params(("parallel",)),
        name="rope_tables",
    )(pos, freq)


def _rope(x, c, s1, s2):
    half = MLA_ROPE // 2
    return x * c + pltpu.roll(x, half, 1) * s1 + pltpu.roll(x, LANES - half, 1) * s2


def _mla_prep_kernel(x_ref, wlat_ref, qnw_ref, kvnw_ref, wuq_ref, wukv_ref,
                     c_ref, s1_ref, s2_ref, q_out, k_out, v_out):
    lat = _dot(x_ref[...].astype(BF16), wlat_ref[...])
    cq = lat[:, :MLA_Q_LORA]
    ckv = lat[:, MLA_Q_LORA:MLA_Q_LORA + MLA_KV_LORA]
    kr = lat[:, MLA_Q_LORA + MLA_KV_LORA:]
    cqn = cq * lax.rsqrt(jnp.mean(cq * cq, axis=-1, keepdims=True) + NORM_EPS) * qnw_ref[...]
    ckvn = ckv * lax.rsqrt(jnp.mean(ckv * ckv, axis=-1, keepdims=True) + NORM_EPS) * kvnw_ref[...]
    q = _dot(cqn.astype(BF16), wuq_ref[...]) * MLA_Q_SCALE
    kv = _dot(ckvn.astype(BF16), wukv_ref[...])
    c, s1, s2 = c_ref[...], s1_ref[...], s2_ref[...]
    kr_rot = _rope(kr, c, s1, s2).astype(BF16)
    for h in range(MLA_HEADS):
        b0 = h * MLA_QK_PAD
        q_out[:, b0:b0 + LANES] = q[:, b0:b0 + LANES].astype(BF16)
        q_out[:, b0 + LANES:b0 + 2 * LANES] = _rope(q[:, b0 + LANES:b0 + 2 * LANES], c, s1, s2).astype(BF16)
        k_out[:, b0:b0 + LANES] = kv[:, b0:b0 + LANES].astype(BF16)
        k_out[:, b0 + LANES:b0 + 2 * LANES] = kr_rot
        v_out[:, h * MLA_V:(h + 1) * MLA_V] = kv[:, b0 + LANES:b0 + 2 * LANES].astype(BF16)


def _mla_prep(x, consts, c, s1, s2):
    n = x.shape[0]
    tm = min(MLA_TM, n)
    row = lambda w: pl.BlockSpec((tm, w), lambda i: (i, 0))
    return pl.pallas_call(
        _mla_prep_kernel,
        grid=(n // tm,),
        in_specs=[row(D_MODEL)] + [spec for _, spec in consts] + [row(LANES)] * 3,
        out_specs=[row(MLA_HEADS * MLA_QK_PAD), row(MLA_HEADS * MLA_QK_PAD), row(MLA_HEADS * MLA_V)],
        out_shape=[jax.ShapeDtypeStruct((n, MLA_HEADS * MLA_QK_PAD), BF16),
                   jax.ShapeDtypeStruct((n, MLA_HEADS * MLA_QK_PAD), BF16),
                   jax.ShapeDtypeStruct((n, MLA_HEADS * MLA_V), BF16)],
        compiler_params=_params(("parallel",)),
        name="mla_prep",
    )(x, *[a for a, _ in consts], c, s1, s2)


def _attention_kernel(q_ref, k_ref, v_ref, o_ref, m_ref, l_ref, acc_ref):
    nb, tq = q_ref.shape[0], q_ref.shape[1]
    i = pl.program_id(1)
    m_ref[...] = jnp.full(m_ref.shape, -1e30, F32)
    l_ref[...] = jnp.zeros(l_ref.shape, F32)
    acc_ref[...] = jnp.zeros(acc_ref.shape, F32)
    r = lax.broadcasted_iota(jnp.int32, (tq, tq), 0)
    c = lax.broadcasted_iota(jnp.int32, (tq, tq), 1)
    pairs = [(b, h) for b in range(nb) for h in range(MLA_HEADS)]
    units = range(len(pairs))
    qk_cols = [slice(h * MLA_QK_PAD, (h + 1) * MLA_QK_PAD) for _, h in pairs]
    v_cols = [slice(h * MLA_V, (h + 1) * MLA_V) for _, h in pairs]

    def kv_block(j, masked):
        rows = pl.ds(pl.multiple_of(j * tq, tq), tq)
        s = [_dot_nt(q_ref[b, :, qk_cols[u]], k_ref[b, rows, qk_cols[u]]) for u, (b, _) in enumerate(pairs)]
        if masked:
            s = [jnp.where(c <= r, s[u], -1e30) for u in units]
        m_prev = [m_ref[u] for u in units]
        m_new = [jnp.maximum(m_prev[u], jnp.max(s[u], axis=-1, keepdims=True)) for u in units]
        p = [jnp.exp2(s[u] - jnp.concatenate([m_new[u]] * (tq // LANES), axis=1)) for u in units]
        alpha = [jnp.exp2(m_prev[u] - m_new[u]) for u in units]
        pv = [_dot(p[u].astype(BF16), v_ref[b, rows, v_cols[u]]) for u, (b, _) in enumerate(pairs)]
        for u in units:
            p_lanes = sum(p[u][:, t * LANES:(t + 1) * LANES] for t in range(1, tq // LANES)) + p[u][:, :LANES]
            l_ref[u] = alpha[u] * l_ref[u] + p_lanes
            acc_ref[u] = alpha[u] * acc_ref[u] + pv[u]
            m_ref[u] = m_new[u]

    def kv_step(j, carry):
        kv_block(j, False)
        return carry

    lax.fori_loop(0, i, kv_step, 0)
    kv_block(i, True)
    for u, (b, _) in enumerate(pairs):
        l = jnp.sum(l_ref[u], axis=-1, keepdims=True)
        o_ref[b, :, v_cols[u]] = (acc_ref[u] / l).astype(o_ref.dtype)


def _attention(q, k, v, batch, seq):
    tq = min(ATT_TQ, seq)
    nb = ATT_NB if batch % ATT_NB == 0 else 1
    qk_w, v_w = MLA_HEADS * MLA_QK_PAD, MLA_HEADS * MLA_V
    stat = pltpu.VMEM((nb * MLA_HEADS, tq, LANES), F32)
    out = pl.pallas_call(
        _attention_kernel,
        grid=(batch // nb, seq // tq),
        in_specs=[pl.BlockSpec((nb, tq, qk_w), lambda b, i: (b, i, 0)),
                  pl.BlockSpec((nb, seq, qk_w), lambda b, i: (b, 0, 0)),
                  pl.BlockSpec((nb, seq, v_w), lambda b, i: (b, 0, 0))],
        out_specs=pl.BlockSpec((nb, tq, v_w), lambda b, i: (b, i, 0)),
        out_shape=jax.ShapeDtypeStruct((batch, seq, v_w), BF16),
        scratch_shapes=[stat, stat, pltpu.VMEM((nb * MLA_HEADS, tq, MLA_V), F32)],
        compiler_params=_params(("parallel", "parallel")),
        name="mla_attention",
    )(q.reshape(batch, seq, qk_w), k.reshape(batch, seq, qk_w), v.reshape(batch, seq, v_w))
    return out.reshape(batch * seq, v_w)


MERGE_TM = 512
MERGE_FC = 256


def _merge_kernel(x_ref, odn_ref, hcv_ref, omla_ref, wg_dn_ref, wg_cv_ref, wg_mla_ref, bgate_ref,
                  wdn_ref, wcv_ref, bcv_ref, wmla_ref, wout_ref, g_ref, b_ref, o_ref, merged_ref):
    x = x_ref[...]
    xb = x.astype(BF16)
    branches = ((odn_ref, wdn_ref, None, wg_dn_ref), (hcv_ref, wcv_ref, bcv_ref, wg_cv_ref),
                (omla_ref, wmla_ref, None, wg_mla_ref))
    for c in range(D_MODEL // MERGE_FC):
        sl = slice(c * MERGE_FC, (c + 1) * MERGE_FC)
        merged = None
        for i, (a_ref, w_ref, bias_ref, wgate_ref) in enumerate(branches):
            gsl = slice(i * D_MODEL + c * MERGE_FC, i * D_MODEL + (c + 1) * MERGE_FC)
            gate = _sigmoid(_dot(xb, wgate_ref[:, sl]) + bgate_ref[:, gsl])
            y = _dot(a_ref[...], w_ref[:, sl])
            if bias_ref is not None:
                y = y + bias_ref[:, sl]
            merged = gate * y if merged is None else merged + gate * y
        merged_ref[:, sl] = merged.astype(BF16)
    for r0 in range(0, x.shape[0], FFN_RB):
        rows = slice(r0, min(r0 + FFN_RB, x.shape[0]))
        y = DEEPNORM_ALPHA * x[rows] + _dot(merged_ref[rows, :], wout_ref[...])
        o_ref[rows, :] = _layer_norm(y, g_ref[...], b_ref[...])


def _merge(x, odn, hcv, omla, consts):
    n = x.shape[0]
    tm = min(MERGE_TM, n)
    row = lambda w: pl.BlockSpec((tm, w), lambda i: (i, 0))
    return pl.pallas_call(
        _merge_kernel,
        grid=(n // tm,),
        in_specs=[row(D_MODEL), row(DN_DIM), row(CV_CHANNELS), row(MLA_HEADS * MLA_V)]
                 + [spec for _, spec in consts],
        out_specs=row(D_MODEL),
        out_shape=jax.ShapeDtypeStruct((n, D_MODEL), F32),
        scratch_shapes=[pltpu.VMEM((tm, D_MODEL), BF16)],
        compiler_params=_params(("parallel",)),
        name="merge",
    )(x, odn, hcv, omla, *[a for a, _ in consts])


_IN_SIZES = (DN_DIM, DN_DIM, DN_DIM, DN_DIM, DN_HEADS, DN_HEADS, 2 * CV_CHANNELS,
             MLA_Q_LORA, MLA_KV_LORA, MLA_ROPE, N_BRANCHES * D_MODEL)


_IN_OFFS = tuple(int(v) for v in np.concatenate([[0], np.cumsum(_IN_SIZES)]))


def _pad_lanes(v, width):
    return jnp.pad(v.astype(F32), ((0, 0), (0, width - v.shape[1])))


def _split_w_in(w_in):
    part = lambda i, j=None: w_in[:, :, _IN_OFFS[i]:_IN_OFFS[(i if j is None else j) + 1]]
    pad = jnp.zeros(w_in.shape[:2] + (LANES - MLA_ROPE,), w_in.dtype)
    return dict(
        latent=jnp.concatenate([part(7, 9), pad], axis=2).astype(BF16),
        wide=jnp.concatenate([part(6), part(10)], axis=2).astype(BF16),
    )


def _pad_w_uq(w_uq):
    depth = w_uq.shape[0]
    w = w_uq.reshape(depth, MLA_Q_LORA, MLA_HEADS, MLA_NOPE + MLA_ROPE)
    w = jnp.pad(w, ((0, 0), (0, 0), (0, 0), (0, MLA_QK_PAD - MLA_NOPE - MLA_ROPE)))
    return w.reshape(depth, MLA_Q_LORA, MLA_HEADS * MLA_QK_PAD).astype(BF16)


def kernel(x, positions, ln1_g, ln1_b, ffn1_w_gate, ffn1_w_up, ffn1_w_down, w_in, b_gate, dn_conv_w, dn_a_log, dn_dt_bias, dn_norm_w, dn_w_o, cv_glu_b, cv_dw_w, cv_dw_b, cv_ln_g, cv_ln_b, cv_w_pw2, cv_b_pw2, mla_q_norm_w, mla_w_uq, mla_kv_norm_w, mla_w_ukv, mla_w_o, w_out, ln2_g, ln2_b, ffn2_w_gate, ffn2_w_up, ffn2_w_down, ln3_g, ln3_b):
    batch, seq, d = x.shape
    n = batch * seq
    h = x.reshape(n, d)

    half = MLA_ROPE // 2
    inv_freq = ROPE_THETA ** (-np.arange(half, dtype=np.float32) / half)
    freq = np.zeros((1, LANES), np.float32)
    freq[0, :half] = inv_freq
    freq[0, half:MLA_ROPE] = inv_freq
    rope_c, rope_s1, rope_s2 = _rope_tables(positions.astype(F32).reshape(n, 1), jnp.asarray(freq))

    bf = lambda a: a.astype(BF16)
    w = _split_w_in(w_in)
    ffn1 = (bf(ffn1_w_gate), bf(ffn1_w_up), bf(ffn1_w_down))
    ffn2 = (bf(ffn2_w_gate), bf(ffn2_w_up), bf(ffn2_w_down))
    dn_w_o_b, cv_w_pw2_b, mla_w_o_b, w_out_b = bf(dn_w_o), bf(cv_w_pw2), bf(mla_w_o), bf(w_out)
    w_uq_b, w_ukv_b = _pad_w_uq(mla_w_uq), bf(mla_w_ukv)
    a_log_p, dt_bias_p = _pad_lanes(dn_a_log, LANES), _pad_lanes(dn_dt_bias, LANES)
    dn_conv_w, cv_dw_w = dn_conv_w.astype(F32), cv_dw_w.astype(F32)

    for l in range(DEPTH):
        h = _ffn_ln(h, [_layer(a, l) for a in ffn1] + [_vec(ln1_g, l), _vec(ln1_b, l)])
        h_cv = _conv_branch(h, batch, seq, [
            _layer(w["wide"], l, 2 * CV_CHANNELS, 0), _vec(cv_glu_b, l), _layer(cv_dw_w, l),
            _vec(cv_dw_b, l), _vec(cv_ln_g, l), _vec(cv_ln_b, l)])
        o_dn = _deltanet(h, batch, seq, [
            _layer(w_in, l, 3 * DN_DIM, 0), _layer(w_in, l, DN_DIM, 3),
            _layer(w_in, l, LANES, _IN_OFFS[4] // LANES), _layer(dn_conv_w, l),
            _vec(a_log_p, l), _vec(dt_bias_p, l), _vec(dn_norm_w, l)])
        q, k, v = _mla_prep(h, [
            _layer(w["latent"], l), _vec(mla_q_norm_w, l), _vec(mla_kv_norm_w, l),
            _layer(w_uq_b, l), _layer(w_ukv_b, l)], rope_c, rope_s1, rope_s2)
        o_mla = _attention(q, k, v, batch, seq)
        h = _merge(h, o_dn, h_cv, o_mla, [
            _layer(w["wide"], l, D_MODEL, 1), _layer(w["wide"], l, D_MODEL, 2), _layer(w["wide"], l, D_MODEL, 3),
            _vec(b_gate, l), _layer(dn_w_o_b, l), _layer(cv_w_pw2_b, l), _vec(cv_b_pw2, l),
            _layer(mla_w_o_b, l), _layer(w_out_b, l), _vec(ln2_g, l), _vec(ln2_b, l)])
        h = _ffn_ln(h, [_layer(a, l) for a in ffn2] + [_vec(ln3_g, l), _vec(ln3_b, l)])
    return h.reshape(batch, seq, d)
```

```python
import math

import jax
import jax.numpy as jnp
import numpy as np
from jax import lax
from jax.experimental import pallas as pl
from jax.experimental.pallas import tpu as pltpu

D_MODEL = 1024
DEPTH = 2
D_FF = 2816
DN_HEADS = 4
DN_HEAD_DIM = 128
DN_DIM = DN_HEADS * DN_HEAD_DIM
DN_CONV = 4
CV_CHANNELS = 512
CV_KERNEL = 31
MLA_HEADS = 4
MLA_Q_LORA = 384
MLA_KV_LORA = 256
MLA_NOPE = 128
MLA_ROPE = 64
MLA_V = 128
ROPE_THETA = 10000.0
N_BRANCHES = 3
DEEPNORM_ALPHA = (2 * DEPTH) ** 0.25
NORM_EPS = 1e-5

LANES = 128
SUBLANES = 8
MLA_QK_PAD = 256
VMEM_LIMIT = 56 * 1024 * 1024

F32 = jnp.float32
BF16 = jnp.bfloat16
HIGHEST = lax.Precision.HIGHEST


def _dot(a, b, precision=None):
    return jnp.dot(a, b, preferred_element_type=F32, precision=precision)


def _dot_nt(a, b, precision=None):
    return lax.dot_general(a, b, (((1,), (1,)), ((), ())), preferred_element_type=F32, precision=precision)


def _dot_tn(a, b):
    return lax.dot_general(a, b, (((0,), (0,)), ((), ())), preferred_element_type=F32)


def _sigmoid(x):
    return 1.0 / (1.0 + jnp.exp(-x))


def _silu(x):
    return x * _sigmoid(x)


def _layer_norm(y, g, b):
    mu = jnp.mean(y, axis=-1, keepdims=True)
    d = y - mu
    var = jnp.mean(d * d, axis=-1, keepdims=True)
    return d * lax.rsqrt(var + NORM_EPS) * g + b


def _resident(arr):
    return arr, pl.BlockSpec(arr.shape, lambda *_: (0, 0), pipeline_mode=pl.Buffered(1))


def _layer(arr, l, width=None, col=0):
    width = arr.shape[2] if width is None else width
    return arr, pl.BlockSpec((None, arr.shape[1], width), lambda *_: (l, 0, col), pipeline_mode=pl.Buffered(1))


def _vec(arr, l):
    return _layer(arr.reshape(arr.shape[0], 1, arr.shape[1]), l)


def _params(sem):
    return pltpu.CompilerParams(dimension_semantics=sem, vmem_limit_bytes=VMEM_LIMIT)


FFN_TM = 1024
FFN_FC = 256
FFN_RB = 256


def _ffn_ln_kernel(x_ref, wg_ref, wu_ref, wd_ref, g_ref, b_ref, o_ref, h_ref):
    x = x_ref[...]
    xb = x.astype(BF16)
    for c in range(D_FF // FFN_FC):
        sl = slice(c * FFN_FC, (c + 1) * FFN_FC)
        gate = _dot(xb, wg_ref[:, sl])
        up = _dot(xb, wu_ref[:, sl])
        h_ref[:, sl] = (_silu(gate) * up).astype(BF16)
    for r0 in range(0, x.shape[0], FFN_RB):
        rows = slice(r0, min(r0 + FFN_RB, x.shape[0]))
        y = DEEPNORM_ALPHA * x[rows] + 0.5 * _dot(h_ref[rows, :], wd_ref[...])
        o_ref[rows, :] = _layer_norm(y, g_ref[...], b_ref[...])


def _ffn_ln(x, consts):
    n = x.shape[0]
    tm = min(FFN_TM, n)
    row = pl.BlockSpec((tm, D_MODEL), lambda i: (i, 0))
    return pl.pallas_call(
        _ffn_ln_kernel,
        grid=(n // tm,),
        in_specs=[row] + [spec for _, spec in consts],
        out_specs=row,
        out_shape=jax.ShapeDtypeStruct((n, D_MODEL), F32),
        scratch_shapes=[pltpu.VMEM((tm, D_FF), BF16)],
        compiler_params=_params(("parallel",)),
        name="ffn_ln",
    )(x, *[a for a, _ in consts])


CV_TT = 256
CV_HALO = 32
CV_RB = 64


def _conv_kernel(x_ref, wglu_ref, glub_ref, dww_ref, dwb_ref, lng_ref, lnb_ref, o_ref,
                 hbuf_ref, shifted_ref, cbuf_ref):
    tt = x_ref.shape[0]

    @pl.when(pl.program_id(1) == 0)
    def _():
        hbuf_ref[0:CV_HALO, :] = jnp.zeros((CV_HALO, CV_CHANNELS), F32)

    u = _dot(x_ref[...].astype(BF16), wglu_ref[...]) + glub_ref[...]
    hbuf_ref[CV_HALO:CV_HALO + tt, :] = u[:, :CV_CHANNELS] * _sigmoid(u[:, CV_CHANNELS:])

    span = CV_HALO + tt - SUBLANES
    for s in range(1, SUBLANES):
        shifted_ref[s - 1] = hbuf_ref[s:s + span, :]

    base = CV_HALO - (CV_KERNEL - 1)
    for rb in range(tt // CV_RB):
        for s in range(CV_CHANNELS // LANES):
            sl = slice(s * LANES, (s + 1) * LANES)
            acc = jnp.zeros((CV_RB, LANES), F32)
            for j in range(CV_KERNEL):
                shift = (base + j) % SUBLANES
                r0 = rb * CV_RB + base + j - shift
                src = hbuf_ref if shift == 0 else shifted_ref.at[shift - 1]
                acc = acc + dww_ref[j:j + 1, sl] * src[r0:r0 + CV_RB, sl]
            cbuf_ref[rb * CV_RB:(rb + 1) * CV_RB, sl] = acc
    tail = hbuf_ref[tt:tt + CV_HALO, :]
    hbuf_ref[0:CV_HALO, :] = tail

    hc = _layer_norm(cbuf_ref[...] + dwb_ref[...], lng_ref[...], lnb_ref[...])
    o_ref[...] = _silu(hc).astype(o_ref.dtype)


def _conv_branch(x, batch, seq, consts):
    n = x.shape[0]
    tt = min(CV_TT, seq)
    nt = seq // tt
    row = lambda w: pl.BlockSpec((tt, w), lambda b, t: (b * nt + t, 0))
    return pl.pallas_call(
        _conv_kernel,
        grid=(batch, nt),
        in_specs=[row(D_MODEL)] + [spec for _, spec in consts],
        out_specs=row(CV_CHANNELS),
        out_shape=jax.ShapeDtypeStruct((n, CV_CHANNELS), BF16),
        scratch_shapes=[pltpu.VMEM((CV_HALO + tt, CV_CHANNELS), F32),
                        pltpu.VMEM((SUBLANES - 1, CV_HALO + tt - SUBLANES, CV_CHANNELS), F32),
                        pltpu.VMEM((tt, CV_CHANNELS), F32)],
        compiler_params=_params(("parallel", "arbitrary")),
        name="conv_branch",
    )(x, *[a for a, _ in consts])


DN_TT = 256
DN_SUB = 64
DN_HALO = SUBLANES
DN_NB = 2


def _deltanet_kernel(x_ref, wqkv_ref, wz_ref, wab_ref, cw_ref, alog_ref, dtb_ref, nw_ref, o_ref,
                     buf_ref, s_ref):
    nb, tt = x_ref.shape[0], x_ref.shape[1]
    seqs = range(nb)

    @pl.when(pl.program_id(1) == 0)
    def _():
        buf_ref[:, 0:DN_HALO, :] = jnp.zeros((nb, DN_HALO, 3 * DN_DIM), F32)
        s_ref[...] = jnp.zeros(s_ref.shape, F32)

    xb = [x_ref[b].astype(BF16) for b in seqs]
    for b in seqs:
        buf_ref[b, DN_HALO:DN_HALO + tt, :] = _dot(xb[b], wqkv_ref[...])
    z = [_dot(xb[b], wz_ref[...]) for b in seqs]
    ab = [_dot(xb[b], wab_ref[...]) for b in seqs]

    qkv = []
    for b in seqs:
        conv = jnp.zeros((tt, 3 * DN_DIM), F32)
        for j in range(DN_CONV):
            off = DN_HALO - (DN_CONV - 1) + j
            conv = conv + cw_ref[j:j + 1, :] * buf_ref[b, off:off + tt, :]
        tail = buf_ref[b, tt:tt + DN_HALO, :]
        buf_ref[b, 0:DN_HALO, :] = tail
        qkv.append(_silu(conv))

    row = lax.broadcasted_iota(jnp.int32, (tt, tt), 0)
    col = lax.broadcasted_iota(jnp.int32, (tt, tt), 1)
    causal = col <= row
    strict = col < row
    same_sub = (row // DN_SUB) == (col // DN_SUB)
    eye = (col == row).astype(F32)
    tril = causal.astype(F32)
    head_lane = lax.broadcasted_iota(jnp.int32, (tt, LANES), 1) < DN_HEADS

    betas, gcs, gcs_t = [], [], []
    for b in seqs:
        sp_in = ab[b] + dtb_ref[...]
        softplus = jnp.maximum(sp_in, 0.0) + jnp.log(1.0 + jnp.exp(-jnp.abs(sp_in)))
        g = jnp.where(head_lane, -jnp.exp(alog_ref[...]) * softplus, 0.0)
        betas.append(_sigmoid(ab[b]))
        gcs.append(_dot(tril, g, HIGHEST))
        gcs_t.append(gcs[b].T)

    pairs = [(b, h) for b in seqs for h in range(DN_HEADS)]
    heads = range(len(pairs))
    hsl = [slice(h * DN_HEAD_DIM, (h + 1) * DN_HEAD_DIM) for _, h in pairs]
    scale = DN_HEAD_DIM ** -0.5

    def unit(x):
        return x * lax.rsqrt(jnp.sum(x * x, axis=-1, keepdims=True) + 1e-6)

    q = [unit(qkv[b][:, h * DN_HEAD_DIM:(h + 1) * DN_HEAD_DIM]) * scale for b, h in pairs]
    k = [unit(qkv[b][:, DN_DIM + h * DN_HEAD_DIM:DN_DIM + (h + 1) * DN_HEAD_DIM]) for b, h in pairs]
    v = [qkv[b][:, 2 * DN_DIM + h * DN_HEAD_DIM:2 * DN_DIM + (h + 1) * DN_HEAD_DIM] for b, h in pairs]
    gc = [gcs[b][:, h:h + 1] for b, h in pairs]
    beta = [betas[b][:, DN_HEADS + h:DN_HEADS + h + 1] for b, h in pairs]
    decay = [jnp.where(causal, jnp.exp(jnp.minimum(gc[u] - gcs_t[b][h:h + 1, :], 0.0)), 0.0)
             for u, (b, h) in enumerate(pairs)]
    kb = [k[h].astype(BF16) for h in heads]
    kk = [_dot_nt(kb[h], kb[h]) for h in heads]
    qk = [_dot_nt(q[h].astype(BF16), kb[h]) for h in heads]
    lmat = [jnp.where(strict, beta[h] * kk[h] * decay[h], 0.0) for h in heads]
    l_diag = [jnp.where(same_sub, lmat[h], 0.0) for h in heads]
    inv_d = [eye - l_diag[h] for h in heads]
    power = [l_diag[h].astype(BF16) for h in heads]
    for _ in range(int(math.log2(DN_SUB)) - 1):
        power = [_dot(power[h], power[h]).astype(BF16) for h in heads]
        inv_d = [inv_d[h] + _dot(inv_d[h].astype(BF16), power[h]) for h in heads]
    inv_db = [inv_d[h].astype(BF16) for h in heads]
    nmat = [_dot(inv_db[h], (lmat[h] - l_diag[h]).astype(BF16)) for h in heads]
    inv_n = [eye - nmat[h] for h in heads]
    power = [nmat[h].astype(BF16) for h in heads]
    for _ in range(int(math.log2(tt // DN_SUB)) - 1):
        power = [_dot(power[h], power[h]).astype(BF16) for h in heads]
        inv_n = [inv_n[h] + _dot(inv_n[h].astype(BF16), power[h]) for h in heads]
    eg = [jnp.exp(gc[h]) for h in heads]
    rhs = [jnp.concatenate([v[h] * beta[h], k[h] * (beta[h] * eg[h])], axis=1).astype(BF16) for h in heads]
    half = [_dot(inv_db[h], rhs[h]).astype(BF16) for h in heads]
    sol = [_dot(inv_n[h].astype(BF16), half[h]) for h in heads]
    qkm = [jnp.where(causal, qk[h] * decay[h], 0.0).astype(BF16) for h in heads]
    g_last = [gc[h][tt - 1:tt, :] for h in heads]
    q_dec = [(q[h] * eg[h]).astype(BF16) for h in heads]
    k_dec = [(k[h] * jnp.exp(g_last[h] - gc[h])).astype(BF16) for h in heads]
    state = [s_ref[h] for h in heads]
    sb = [state[h].astype(BF16) for h in heads]
    vb = [(sol[h][:, :DN_HEAD_DIM] - _dot(sol[h][:, DN_HEAD_DIM:].astype(BF16), sb[h])).astype(BF16)
          for h in heads]
    o = [_dot(q_dec[h], sb[h]) + _dot(qkm[h], vb[h]) for h in heads]
    for h in heads:
        s_ref[h] = state[h] * jnp.exp(g_last[h]) + _dot_tn(k_dec[h], vb[h])
    for u, (b, _) in enumerate(pairs):
        on = o[u] * lax.rsqrt(jnp.mean(o[u] * o[u], axis=-1, keepdims=True) + NORM_EPS) * nw_ref[...]
        o_ref[b, :, hsl[u]] = (on * _silu(z[b][:, hsl[u]])).astype(o_ref.dtype)


def _deltanet(x, batch, seq, consts):
    tt = min(DN_TT, seq)
    nb = DN_NB if batch % DN_NB == 0 else 1
    block = lambda w: pl.BlockSpec((nb, tt, w), lambda i, t: (i, t, 0))
    out = pl.pallas_call(
        _deltanet_kernel,
        grid=(batch // nb, seq // tt),
        in_specs=[block(D_MODEL)] + [spec for _, spec in consts],
        out_specs=block(DN_DIM),
        out_shape=jax.ShapeDtypeStruct((batch, seq, DN_DIM), BF16),
        scratch_shapes=[
            pltpu.VMEM((nb, DN_HALO + tt, 3 * DN_DIM), F32),
            pltpu.VMEM((nb * DN_HEADS, DN_HEAD_DIM, DN_HEAD_DIM), F32),
        ],
        compiler_params=_params(("parallel", "arbitrary")),
        name="deltanet",
    )(x.reshape(batch, seq, D_MODEL), *[a for a, _ in consts])
    return out.reshape(batch * seq, DN_DIM)


ROPE_TM = 512
MLA_TM = 512
ATT_TQ = 256
ATT_NB = 2
MLA_Q_SCALE = (MLA_NOPE + MLA_ROPE) ** -0.5 * math.log2(math.e)


def _rope_table_kernel(pos_ref, freq_ref, c_ref, s1_ref, s2_ref):
    half = MLA_ROPE // 2
    ang = pos_ref[...] * freq_ref[...]
    lane = lax.broadcasted_iota(jnp.int32, ang.shape, 1)
    cos = jnp.cos(ang)
    sin = jnp.sin(ang)
    c_ref[...] = jnp.where(lane < MLA_ROPE, cos, 0.0)
    s1_ref[...] = jnp.where((lane >= half) & (lane < MLA_ROPE), sin, 0.0)
    s2_ref[...] = jnp.where(lane < half, -sin, 0.0)


def _rope_tables(pos, freq):
    n = pos.shape[0]
    tm = min(ROPE_TM, n)
    out = pl.BlockSpec((tm, LANES), lambda i: (i, 0))
    return pl.pallas_call(
        _rope_table_kernel,
        grid=(n // tm,),
        in_specs=[pl.BlockSpec((tm, 1), lambda i: (i, 0)), _resident(freq)[1]],
        out_specs=[out, out, out],
        out_shape=[jax.ShapeDtypeStruct((n, LANES), F32)] * 3,
        compiler_params=_params(("parallel",)),
        name="rope_tables",
    )(pos, freq)


def _rope(x, c, s1, s2):
    half = MLA_ROPE // 2
    return x * c + pltpu.roll(x, half, 1) * s1 + pltpu.roll(x, LANES - half, 1) * s2


def _mla_prep_kernel(x_ref, wlat_ref, qnw_ref, kvnw_ref, wuq_ref, wukv_ref,
                     c_ref, s1_ref, s2_ref, q_out, k_out, v_out):
    lat = _dot(x_ref[...].astype(BF16), wlat_ref[...])
    cq = lat[:, :MLA_Q_LORA]
    ckv = lat[:, MLA_Q_LORA:MLA_Q_LORA + MLA_KV_LORA]
    kr = lat[:, MLA_Q_LORA + MLA_KV_LORA:]
    cqn = cq * lax.rsqrt(jnp.mean(cq * cq, axis=-1, keepdims=True) + NORM_EPS) * qnw_ref[...]
    ckvn = ckv * lax.rsqrt(jnp.mean(ckv * ckv, axis=-1, keepdims=True) + NORM_EPS) * kvnw_ref[...]
    q = _dot(cqn.astype(BF16), wuq_ref[...]) * MLA_Q_SCALE
    kv = _dot(ckvn.astype(BF16), wukv_ref[...])
    c, s1, s2 = c_ref[...], s1_ref[...], s2_ref[...]
    kr_rot = _rope(kr, c, s1, s2).astype(BF16)
    for h in range(MLA_HEADS):
        b0 = h * MLA_QK_PAD
        q_out[:, b0:b0 + LANES] = q[:, b0:b0 + LANES].astype(BF16)
        q_out[:, b0 + LANES:b0 + 2 * LANES] = _rope(q[:, b0 + LANES:b0 + 2 * LANES], c, s1, s2).astype(BF16)
        k_out[:, b0:b0 + LANES] = kv[:, b0:b0 + LANES].astype(BF16)
        k_out[:, b0 + LANES:b0 + 2 * LANES] = kr_rot
        v_out[:, h * MLA_V:(h + 1) * MLA_V] = kv[:, b0 + LANES:b0 + 2 * LANES].astype(BF16)


def _mla_prep(x, consts, c, s1, s2):
    n = x.shape[0]
    tm = min(MLA_TM, n)
    row = lambda w: pl.BlockSpec((tm, w), lambda i: (i, 0))
    return pl.pallas_call(
        _mla_prep_kernel,
        grid=(n // tm,),
        in_specs=[row(D_MODEL)] + [spec for _, spec in consts] + [row(LANES)] * 3,
        out_specs=[row(MLA_HEADS * MLA_QK_PAD), row(MLA_HEADS * MLA_QK_PAD), row(MLA_HEADS * MLA_V)],
        out_shape=[jax.ShapeDtypeStruct((n, MLA_HEADS * MLA_QK_PAD), BF16),
                   jax.ShapeDtypeStruct((n, MLA_HEADS * MLA_QK_PAD), BF16),
                   jax.ShapeDtypeStruct((n, MLA_HEADS * MLA_V), BF16)],
        compiler_params=_params(("parallel",)),
        name="mla_prep",
    )(x, *[a for a, _ in consts], c, s1, s2)


def _attention_kernel(q_ref, k_ref, v_ref, o_ref, m_ref, l_ref, acc_ref):
    nb, tq = q_ref.shape[0], q_ref.shape[1]
    i = pl.program_id(1)
    m_ref[...] = jnp.full(m_ref.shape, -1e30, F32)
    l_ref[...] = jnp.zeros(l_ref.shape, F32)
    acc_ref[...] = jnp.zeros(acc_ref.shape, F32)
    r = lax.broadcasted_iota(jnp.int32, (tq, tq), 0)
    c = lax.broadcasted_iota(jnp.int32, (tq, tq), 1)
    pairs = [(b, h) for b in range(nb) for h in range(MLA_HEADS)]
    units = range(len(pairs))
    qk_cols = [slice(h * MLA_QK_PAD, (h + 1) * MLA_QK_PAD) for _, h in pairs]
    v_cols = [slice(h * MLA_V, (h + 1) * MLA_V) for _, h in pairs]

    def kv_block(j, masked):
        rows = pl.ds(pl.multiple_of(j * tq, tq), tq)
        s = [_dot_nt(q_ref[b, :, qk_cols[u]], k_ref[b, rows, qk_cols[u]]) for u, (b, _) in enumerate(pairs)]
        if masked:
            s = [jnp.where(c <= r, s[u], -1e30) for u in units]
        m_prev = [m_ref[u] for u in units]
        m_new = [jnp.maximum(m_prev[u], jnp.max(s[u], axis=-1, keepdims=True)) for u in units]
        p = [jnp.exp2(s[u] - jnp.concatenate([m_new[u]] * (tq // LANES), axis=1)) for u in units]
        alpha = [jnp.exp2(m_prev[u] - m_new[u]) for u in units]
        pv = [_dot(p[u].astype(BF16), v_ref[b, rows, v_cols[u]]) for u, (b, _) in enumerate(pairs)]
        for u in units:
            p_lanes = sum(p[u][:, t * LANES:(t + 1) * LANES] for t in range(1, tq // LANES)) + p[u][:, :LANES]
            l_ref[u] = alpha[u] * l_ref[u] + p_lanes
            acc_ref[u] = alpha[u] * acc_ref[u] + pv[u]
            m_ref[u] = m_new[u]

    def kv_step(j, carry):
        kv_block(j, False)
        return carry

    lax.fori_loop(0, i, kv_step, 0)
    kv_block(i, True)
    for u, (b, _) in enumerate(pairs):
        l = jnp.sum(l_ref[u], axis=-1, keepdims=True)
        o_ref[b, :, v_cols[u]] = (acc_ref[u] / l).astype(o_ref.dtype)


def _attention(q, k, v, batch, seq):
    tq = min(ATT_TQ, seq)
    nb = ATT_NB if batch % ATT_NB == 0 else 1
    qk_w, v_w = MLA_HEADS * MLA_QK_PAD, MLA_HEADS * MLA_V
    stat = pltpu.VMEM((nb * MLA_HEADS, tq, LANES), F32)
    out = pl.pallas_call(
        _attention_kernel,
        grid=(batch // nb, seq // tq),
        in_specs=[pl.BlockSpec((nb, tq, qk_w), lambda b, i: (b, i, 0)),
                  pl.BlockSpec((nb, seq, qk_w), lambda b, i: (b, 0, 0)),
                  pl.BlockSpec((nb, seq, v_w), lambda b, i: (b, 0, 0))],
        out_specs=pl.BlockSpec((nb, tq, v_w), lambda b, i: (b, i, 0)),
        out_shape=jax.ShapeDtypeStruct((batch, seq, v_w), BF16),
        scratch_shapes=[stat, stat, pltpu.VMEM((nb * MLA_HEADS, tq, MLA_V), F32)],
        compiler_params=_params(("parallel", "parallel")),
        name="mla_attention",
    )(q.reshape(batch, seq, qk_w), k.reshape(batch, seq, qk_w), v.reshape(batch, seq, v_w))
    return out.reshape(batch * seq, v_w)


MERGE_TM = 512
MERGE_FC = 256


def _merge_kernel(x_ref, odn_ref, hcv_ref, omla_ref, wg_dn_ref, wg_cv_ref, wg_mla_ref, bgate_ref,
                  wdn_ref, wcv_ref, bcv_ref, wmla_ref, wout_ref, g_ref, b_ref, o_ref, merged_ref):
    x = x_ref[...]
    xb = x.astype(BF16)
    branches = ((odn_ref, wdn_ref, None, wg_dn_ref), (hcv_ref, wcv_ref, bcv_ref, wg_cv_ref),
                (omla_ref, wmla_ref, None, wg_mla_ref))
    for c in range(D_MODEL // MERGE_FC):
        sl = slice(c * MERGE_FC, (c + 1) * MERGE_FC)
        merged = None
        for i, (a_ref, w_ref, bias_ref, wgate_ref) in enumerate(branches):
            gsl = slice(i * D_MODEL + c * MERGE_FC, i * D_MODEL + (c + 1) * MERGE_FC)
            gate = _sigmoid(_dot(xb, wgate_ref[:, sl]) + bgate_ref[:, gsl])
            y = _dot(a_ref[...], w_ref[:, sl])
            if bias_ref is not None:
                y = y + bias_ref[:, sl]
            merged = gate * y if merged is None else merged + gate * y
        merged_ref[:, sl] = merged.astype(BF16)
    for r0 in range(0, x.shape[0], FFN_RB):
        rows = slice(r0, min(r0 + FFN_RB, x.shape[0]))
        y = DEEPNORM_ALPHA * x[rows] + _dot(merged_ref[rows, :], wout_ref[...])
        o_ref[rows, :] = _layer_norm(y, g_ref[...], b_ref[...])


def _merge(x, odn, hcv, omla, consts):
    n = x.shape[0]
    tm = min(MERGE_TM, n)
    row = lambda w: pl.BlockSpec((tm, w), lambda i: (i, 0))
    return pl.pallas_call(
        _merge_kernel,
        grid=(n // tm,),
        in_specs=[row(D_MODEL), row(DN_DIM), row(CV_CHANNELS), row(MLA_HEADS * MLA_V)]
                 + [spec for _, spec in consts],
        out_specs=row(D_MODEL),
        out_shape=jax.ShapeDtypeStruct((n, D_MODEL), F32),
        scratch_shapes=[pltpu.VMEM((tm, D_MODEL), BF16)],
        compiler_params=_params(("parallel",)),
        name="merge",
    )(x, odn, hcv, omla, *[a for a, _ in consts])


_IN_SIZES = (DN_DIM, DN_DIM, DN_DIM, DN_DIM, DN_HEADS, DN_HEADS, 2 * CV_CHANNELS,
             MLA_Q_LORA, MLA_KV_LORA, MLA_ROPE, N_BRANCHES * D_MODEL)


_IN_OFFS = tuple(int(v) for v in np.concatenate([[0], np.cumsum(_IN_SIZES)]))
_DN_COLS = _IN_OFFS[4] + LANES


def _pad_lanes(v, width):
    return jnp.pad(v.astype(F32), ((0, 0), (0, width - v.shape[1])))


def _split_w_in(w_in):
    part = lambda i, j=None: w_in[:, :, _IN_OFFS[i]:_IN_OFFS[(i if j is None else j) + 1]]
    pad = jnp.zeros(w_in.shape[:2] + (LANES - MLA_ROPE,), w_in.dtype)
    return dict(
        dn=w_in[:, :, :_DN_COLS].astype(BF16),
        latent=jnp.concatenate([part(7, 9), pad], axis=2).astype(BF16),
        wide=jnp.concatenate([part(6), part(10)], axis=2).astype(BF16),
    )


def _pad_w_uq(w_uq):
    depth = w_uq.shape[0]
    w = w_uq.reshape(depth, MLA_Q_LORA, MLA_HEADS, MLA_NOPE + MLA_ROPE)
    w = jnp.pad(w, ((0, 0), (0, 0), (0, 0), (0, MLA_QK_PAD - MLA_NOPE - MLA_ROPE)))
    return w.reshape(depth, MLA_Q_LORA, MLA_HEADS * MLA_QK_PAD).astype(BF16)


def kernel(x, positions, ln1_g, ln1_b, ffn1_w_gate, ffn1_w_up, ffn1_w_down, w_in, b_gate, dn_conv_w, dn_a_log, dn_dt_bias, dn_norm_w, dn_w_o, cv_glu_b, cv_dw_w, cv_dw_b, cv_ln_g, cv_ln_b, cv_w_pw2, cv_b_pw2, mla_q_norm_w, mla_w_uq, mla_kv_norm_w, mla_w_ukv, mla_w_o, w_out, ln2_g, ln2_b, ffn2_w_gate, ffn2_w_up, ffn2_w_down, ln3_g, ln3_b):
    batch, seq, d = x.shape
    n = batch * seq
    h = x.reshape(n, d)

    half = MLA_ROPE // 2
    inv_freq = ROPE_THETA ** (-np.arange(half, dtype=np.float32) / half)
    freq = np.zeros((1, LANES), np.float32)
    freq[0, :half] = inv_freq
    freq[0, half:MLA_ROPE] = inv_freq
    rope_c, rope_s1, rope_s2 = _rope_tables(positions.astype(F32).reshape(n, 1), jnp.asarray(freq))

    bf = lambda a: a.astype(BF16)
    w = _split_w_in(w_in)
    ffn1 = (bf(ffn1_w_gate), bf(ffn1_w_up), bf(ffn1_w_down))
    ffn2 = (bf(ffn2_w_gate), bf(ffn2_w_up), bf(ffn2_w_down))
    dn_w_o_b, cv_w_pw2_b, mla_w_o_b, w_out_b = bf(dn_w_o), bf(cv_w_pw2), bf(mla_w_o), bf(w_out)
    w_uq_b, w_ukv_b = _pad_w_uq(mla_w_uq), bf(mla_w_ukv)
    a_log_p, dt_bias_p = _pad_lanes(dn_a_log, LANES), _pad_lanes(dn_dt_bias, LANES)
    dn_conv_w, cv_dw_w = dn_conv_w.astype(F32), cv_dw_w.astype(F32)

    for l in range(DEPTH):
        h = _ffn_ln(h, [_layer(a, l) for a in ffn1] + [_vec(ln1_g, l), _vec(ln1_b, l)])
        h_cv = _conv_branch(h, batch, seq, [
            _layer(w["wide"], l, 2 * CV_CHANNELS, 0), _vec(cv_glu_b, l), _layer(cv_dw_w, l),
            _vec(cv_dw_b, l), _vec(cv_ln_g, l), _vec(cv_ln_b, l)])
        o_dn = _deltanet(h, batch, seq, [
            _layer(w["dn"], l, 3 * DN_DIM, 0), _layer(w["dn"], l, DN_DIM, 3),
            _layer(w["dn"], l, LANES, _IN_OFFS[4] // LANES), _layer(dn_conv_w, l),
            _vec(a_log_p, l), _vec(dt_bias_p, l), _vec(dn_norm_w, l)])
        q, k, v = _mla_prep(h, [
            _layer(w["latent"], l), _vec(mla_q_norm_w, l), _vec(mla_kv_norm_w, l),
            _layer(w_uq_b, l), _layer(w_ukv_b, l)], rope_c, rope_s1, rope_s2)
        o_mla = _attention(q, k, v, batch, seq)
        h = _merge(h, o_dn, h_cv, o_mla, [
            _layer(w["wide"], l, D_MODEL, 1), _layer(w["wide"], l, D_MODEL, 2), _layer(w["wide"], l, D_MODEL, 3),
            _vec(b_gate, l), _layer(dn_w_o_b, l), _layer(cv_w_pw2_b, l), _vec(cv_b_pw2, l),
            _layer(mla_w_o_b, l), _layer(w_out_b, l), _vec(ln2_g, l), _vec(ln2_b, l)])
        h = _ffn_ln(h, [_layer(a, l) for a in ffn2] + [_vec(ln3_g, l), _vec(ln3_b, l)])
    return h.reshape(batch, seq, d)
```

```python
import math

import jax
import jax.numpy as jnp
import numpy as np
from jax import lax
from jax.experimental import pallas as pl
from jax.experimental.pallas import tpu as pltpu

D_MODEL = 1024
DEPTH = 2
D_FF = 2816
DN_HEADS = 4
DN_HEAD_DIM = 128
DN_DIM = DN_HEADS * DN_HEAD_DIM
DN_CONV = 4
CV_CHANNELS = 512
CV_KERNEL = 31
MLA_HEADS = 4
MLA_Q_LORA = 384
MLA_KV_LORA = 256
MLA_NOPE = 128
MLA_ROPE = 64
MLA_V = 128
ROPE_THETA = 10000.0
N_BRANCHES = 3
DEEPNORM_ALPHA = (2 * DEPTH) ** 0.25
NORM_EPS = 1e-5

LANES = 128
SUBLANES = 8
MLA_QK_PAD = 256
VMEM_LIMIT = 56 * 1024 * 1024

F32 = jnp.float32
BF16 = jnp.bfloat16
HIGHEST = lax.Precision.HIGHEST


def _dot(a, b, precision=None):
    return jnp.dot(a, b, preferred_element_type=F32, precision=precision)


def _dot_nt(a, b, precision=None):
    return lax.dot_general(a, b, (((1,), (1,)), ((), ())), preferred_element_type=F32, precision=precision)


def _dot_tn(a, b):
    return lax.dot_general(a, b, (((0,), (0,)), ((), ())), preferred_element_type=F32)


def _sigmoid(x):
    return 1.0 / (1.0 + jnp.exp(-x))


def _silu(x):
    return x * _sigmoid(x)


def _layer_norm(y, g, b):
    mu = jnp.mean(y, axis=-1, keepdims=True)
    d = y - mu
    var = jnp.mean(d * d, axis=-1, keepdims=True)
    return d * lax.rsqrt(var + NORM_EPS) * g + b


def _resident(arr):
    return arr, pl.BlockSpec(arr.shape, lambda *_: (0, 0), pipeline_mode=pl.Buffered(1))


def _layer(arr, l, width=None, col=0):
    width = arr.shape[2] if width is None else width
    return arr, pl.BlockSpec((None, arr.shape[1], width), lambda *_: (l, 0, col), pipeline_mode=pl.Buffered(1))


def _vec(arr, l):
    return _layer(arr.reshape(arr.shape[0], 1, arr.shape[1]), l)


def _params(sem):
    return pltpu.CompilerParams(dimension_semantics=sem, vmem_limit_bytes=VMEM_LIMIT)


FFN_TM = 1024
FFN_FC = 256
FFN_RB = 256


def _ffn_ln_kernel(x_ref, wg_ref, wu_ref, wd_ref, g_ref, b_ref, o_ref, h_ref):
    x = x_ref[...]
    xb = x.astype(BF16)
    for c in range(D_FF // FFN_FC):
        sl = slice(c * FFN_FC, (c + 1) * FFN_FC)
        gate = _dot(xb, wg_ref[:, sl])
        up = _dot(xb, wu_ref[:, sl])
        h_ref[:, sl] = (_silu(gate) * up).astype(BF16)
    for r0 in range(0, x.shape[0], FFN_RB):
        rows = slice(r0, min(r0 + FFN_RB, x.shape[0]))
        y = DEEPNORM_ALPHA * x[rows] + 0.5 * _dot(h_ref[rows, :], wd_ref[...])
        o_ref[rows, :] = _layer_norm(y, g_ref[...], b_ref[...])


def _ffn_ln(x, consts):
    n = x.shape[0]
    tm = min(FFN_TM, n)
    row = pl.BlockSpec((tm, D_MODEL), lambda i: (i, 0))
    return pl.pallas_call(
        _ffn_ln_kernel,
        grid=(n // tm,),
        in_specs=[row] + [spec for _, spec in consts],
        out_specs=row,
        out_shape=jax.ShapeDtypeStruct((n, D_MODEL), F32),
        scratch_shapes=[pltpu.VMEM((tm, D_FF), BF16)],
        compiler_params=_params(("parallel",)),
        name="ffn_ln",
    )(x, *[a for a, _ in consts])


DN_TT = 256
DN_SUB = 64
DN_HALO = SUBLANES
DN_NB = 2


def _deltanet_kernel(x_ref, wqkv_ref, wz_ref, wab_ref, cw_ref, alog_ref, dtb_ref, nw_ref, o_ref,
                     buf_ref, s_ref):
    nb, tt = x_ref.shape[0], x_ref.shape[1]
    seqs = range(nb)

    @pl.when(pl.program_id(1) == 0)
    def _():
        buf_ref[:, 0:DN_HALO, :] = jnp.zeros((nb, DN_HALO, 3 * DN_DIM), F32)
        s_ref[...] = jnp.zeros(s_ref.shape, F32)

    xb = [x_ref[b].astype(BF16) for b in seqs]
    for b in seqs:
        buf_ref[b, DN_HALO:DN_HALO + tt, :] = _dot(xb[b], wqkv_ref[...])
    z = [_dot(xb[b], wz_ref[...]) for b in seqs]
    ab = [_dot(xb[b], wab_ref[...]) for b in seqs]

    qkv = []
    for b in seqs:
        conv = jnp.zeros((tt, 3 * DN_DIM), F32)
        for j in range(DN_CONV):
            off = DN_HALO - (DN_CONV - 1) + j
            conv = conv + cw_ref[j:j + 1, :] * buf_ref[b, off:off + tt, :]
        tail = buf_ref[b, tt:tt + DN_HALO, :]
        buf_ref[b, 0:DN_HALO, :] = tail
        qkv.append(_silu(conv))

    row = lax.broadcasted_iota(jnp.int32, (tt, tt), 0)
    col = lax.broadcasted_iota(jnp.int32, (tt, tt), 1)
    causal = col <= row
    strict = col < row
    same_sub = (row // DN_SUB) == (col // DN_SUB)
    eye = (col == row).astype(F32)
    tril = causal.astype(F32)
    head_lane = lax.broadcasted_iota(jnp.int32, (tt, LANES), 1) < DN_HEADS

    betas, gcs, gcs_t = [], [], []
    for b in seqs:
        sp_in = ab[b] + dtb_ref[...]
        softplus = jnp.maximum(sp_in, 0.0) + jnp.log(1.0 + jnp.exp(-jnp.abs(sp_in)))
        g = jnp.where(head_lane, -jnp.exp(alog_ref[...]) * softplus, 0.0)
        betas.append(_sigmoid(ab[b]))
        gcs.append(_dot(tril, g, HIGHEST))
        gcs_t.append(gcs[b].T)

    pairs = [(b, h) for b in seqs for h in range(DN_HEADS)]
    heads = range(len(pairs))
    hsl = [slice(h * DN_HEAD_DIM, (h + 1) * DN_HEAD_DIM) for _, h in pairs]
    scale = DN_HEAD_DIM ** -0.5

    def unit(x):
        return x * lax.rsqrt(jnp.sum(x * x, axis=-1, keepdims=True) + 1e-6)

    q = [unit(qkv[b][:, h * DN_HEAD_DIM:(h + 1) * DN_HEAD_DIM]) * scale for b, h in pairs]
    k = [unit(qkv[b][:, DN_DIM + h * DN_HEAD_DIM:DN_DIM + (h + 1) * DN_HEAD_DIM]) for b, h in pairs]
    v = [qkv[b][:, 2 * DN_DIM + h * DN_HEAD_DIM:2 * DN_DIM + (h + 1) * DN_HEAD_DIM] for b, h in pairs]
    gc = [gcs[b][:, h:h + 1] for b, h in pairs]
    beta = [betas[b][:, DN_HEADS + h:DN_HEADS + h + 1] for b, h in pairs]
    decay = [jnp.where(causal, jnp.exp(jnp.minimum(gc[u] - gcs_t[b][h:h + 1, :], 0.0)), 0.0)
             for u, (b, h) in enumerate(pairs)]
    kb = [k[h].astype(BF16) for h in heads]
    kk = [_dot_nt(kb[h], kb[h]) for h in heads]
    qk = [_dot_nt(q[h].astype(BF16), kb[h]) for h in heads]
    lmat = [jnp.where(strict, beta[h] * kk[h] * decay[h], 0.0) for h in heads]
    l_diag = [jnp.where(same_sub, lmat[h], 0.0) for h in heads]
    inv_d = [eye - l_diag[h] for h in heads]
    power = [l_diag[h].astype(BF16) for h in heads]
    for _ in range(int(math.log2(DN_SUB)) - 1):
        power = [_dot(power[h], power[h]).astype(BF16) for h in heads]
        inv_d = [inv_d[h] + _dot(inv_d[h].astype(BF16), power[h]) for h in heads]
    inv_db = [inv_d[h].astype(BF16) for h in heads]
    nmat = [_dot(inv_db[h], (lmat[h] - l_diag[h]).astype(BF16)) for h in heads]
    inv_n = [eye - nmat[h] for h in heads]
    power = [nmat[h].astype(BF16) for h in heads]
    for _ in range(int(math.log2(tt // DN_SUB)) - 1):
        power = [_dot(power[h], power[h]).astype(BF16) for h in heads]
        inv_n = [inv_n[h] + _dot(inv_n[h].astype(BF16), power[h]) for h in heads]
    eg = [jnp.exp(gc[h]) for h in heads]
    rhs = [jnp.concatenate([v[h] * beta[h], k[h] * (beta[h] * eg[h])], axis=1).astype(BF16) for h in heads]
    half = [_dot(inv_db[h], rhs[h]).astype(BF16) for h in heads]
    sol = [_dot(inv_n[h].astype(BF16), half[h]) for h in heads]
    qkm = [jnp.where(causal, qk[h] * decay[h], 0.0).astype(BF16) for h in heads]
    g_last = [gc[h][tt - 1:tt, :] for h in heads]
    q_dec = [(q[h] * eg[h]).astype(BF16) for h in heads]
    k_dec = [(k[h] * jnp.exp(g_last[h] - gc[h])).astype(BF16) for h in heads]
    state = [s_ref[h] for h in heads]
    sb = [state[h].astype(BF16) for h in heads]
    vb = [(sol[h][:, :DN_HEAD_DIM] - _dot(sol[h][:, DN_HEAD_DIM:].astype(BF16), sb[h])).astype(BF16)
          for h in heads]
    o = [_dot(q_dec[h], sb[h]) + _dot(qkm[h], vb[h]) for h in heads]
    for h in heads:
        s_ref[h] = state[h] * jnp.exp(g_last[h]) + _dot_tn(k_dec[h], vb[h])
    for u, (b, _) in enumerate(pairs):
        on = o[u] * lax.rsqrt(jnp.mean(o[u] * o[u], axis=-1, keepdims=True) + NORM_EPS) * nw_ref[...]
        o_ref[b, :, hsl[u]] = (on * _silu(z[b][:, hsl[u]])).astype(o_ref.dtype)


def _deltanet(x, batch, seq, consts):
    tt = min(DN_TT, seq)
    nb = DN_NB if batch % DN_NB == 0 else 1
    block = lambda w: pl.BlockSpec((nb, tt, w), lambda i, t: (i, t, 0))
    out = pl.pallas_call(
        _deltanet_kernel,
        grid=(batch // nb, seq // tt),
        in_specs=[block(D_MODEL)] + [spec for _, spec in consts],
        out_specs=block(DN_DIM),
        out_shape=jax.ShapeDtypeStruct((batch, seq, DN_DIM), BF16),
        scratch_shapes=[
            pltpu.VMEM((nb, DN_HALO + tt, 3 * DN_DIM), F32),
            pltpu.VMEM((nb * DN_HEADS, DN_HEAD_DIM, DN_HEAD_DIM), F32),
        ],
        compiler_params=_params(("parallel", "arbitrary")),
        name="deltanet",
    )(x.reshape(batch, seq, D_MODEL), *[a for a, _ in consts])
    return out.reshape(batch * seq, DN_DIM)


ROPE_TM = 512
MLA_TM = 512
ATT_TQ = 256
ATT_NB = 2
MLA_Q_SCALE = (MLA_NOPE + MLA_ROPE) ** -0.5 * math.log2(math.e)


def _rope_table_kernel(pos_ref, freq_ref, c_ref, s1_ref, s2_ref):
    half = MLA_ROPE // 2
    ang = pos_ref[...] * freq_ref[...]
    lane = lax.broadcasted_iota(jnp.int32, ang.shape, 1)
    cos = jnp.cos(ang)
    sin = jnp.sin(ang)
    c_ref[...] = jnp.where(lane < MLA_ROPE, cos, 0.0)
    s1_ref[...] = jnp.where((lane >= half) & (lane < MLA_ROPE), sin, 0.0)
    s2_ref[...] = jnp.where(lane < half, -sin, 0.0)


def _rope_tables(pos, freq):
    n = pos.shape[0]
    tm = min(ROPE_TM, n)
    out = pl.BlockSpec((tm, LANES), lambda i: (i, 0))
    return pl.pallas_call(
        _rope_table_kernel,
        grid=(n // tm,),
        in_specs=[pl.BlockSpec((tm, 1), lambda i: (i, 0)), _resident(freq)[1]],
        out_specs=[out, out, out],
        out_shape=[jax.ShapeDtypeStruct((n, LANES), F32)] * 3,
        compiler_params=_params(("parallel",)),
        name="rope_tables",
    )(pos, freq)


def _rope(x, c, s1, s2):
    half = MLA_ROPE // 2
    return x * c + pltpu.roll(x, half, 1) * s1 + pltpu.roll(x, LANES - half, 1) * s2


def _mla_prep_kernel(x_ref, wlat_ref, qnw_ref, kvnw_ref, wuq_ref, wukv_ref,
                     c_ref, s1_ref, s2_ref, q_out, k_out, v_out):
    lat = _dot(x_ref[...].astype(BF16), wlat_ref[...])
    cq = lat[:, :MLA_Q_LORA]
    ckv = lat[:, MLA_Q_LORA:MLA_Q_LORA + MLA_KV_LORA]
    kr = lat[:, MLA_Q_LORA + MLA_KV_LORA:]
    cqn = cq * lax.rsqrt(jnp.mean(cq * cq, axis=-1, keepdims=True) + NORM_EPS) * qnw_ref[...]
    ckvn = ckv * lax.rsqrt(jnp.mean(ckv * ckv, axis=-1, keepdims=True) + NORM_EPS) * kvnw_ref[...]
    q = _dot(cqn.astype(BF16), wuq_ref[...]) * MLA_Q_SCALE
    kv = _dot(ckvn.astype(BF16), wukv_ref[...])
    c, s1, s2 = c_ref[...], s1_ref[...], s2_ref[...]
    kr_rot = _rope(kr, c, s1, s2).astype(BF16)
    for h in range(MLA_HEADS):
        b0 = h * MLA_QK_PAD
        q_out[:, b0:b0 + LANES] = q[:, b0:b0 + LANES].astype(BF16)
        q_out[:, b0 + LANES:b0 + 2 * LANES] = _rope(q[:, b0 + LANES:b0 + 2 * LANES], c, s1, s2).astype(BF16)
        k_out[:, b0:b0 + LANES] = kv[:, b0:b0 + LANES].astype(BF16)
        k_out[:, b0 + LANES:b0 + 2 * LANES] = kr_rot
        v_out[:, h * MLA_V:(h + 1) * MLA_V] = kv[:, b0 + LANES:b0 + 2 * LANES].astype(BF16)


def _mla_prep(x, consts, c, s1, s2):
    n = x.shape[0]
    tm = min(MLA_TM, n)
    row = lambda w: pl.BlockSpec((tm, w), lambda i: (i, 0))
    return pl.pallas_call(
        _mla_prep_kernel,
        grid=(n // tm,),
        in_specs=[row(D_MODEL)] + [spec for _, spec in consts] + [row(LANES)] * 3,
        out_specs=[row(MLA_HEADS * MLA_QK_PAD), row(MLA_HEADS * MLA_QK_PAD), row(MLA_HEADS * MLA_V)],
        out_shape=[jax.ShapeDtypeStruct((n, MLA_HEADS * MLA_QK_PAD), BF16),
                   jax.ShapeDtypeStruct((n, MLA_HEADS * MLA_QK_PAD), BF16),
                   jax.ShapeDtypeStruct((n, MLA_HEADS * MLA_V), BF16)],
        compiler_params=_params(("parallel",)),
        name="mla_prep",
    )(x, *[a for a, _ in consts], c, s1, s2)


def _attention_kernel(q_ref, k_ref, v_ref, o_ref, m_ref, l_ref, acc_ref):
    nb, tq = q_ref.shape[0], q_ref.shape[1]
    i = pl.program_id(1)
    m_ref[...] = jnp.full(m_ref.shape, -1e30, F32)
    l_ref[...] = jnp.zeros(l_ref.shape, F32)
    acc_ref[...] = jnp.zeros(acc_ref.shape, F32)
    r = lax.broadcasted_iota(jnp.int32, (tq, tq), 0)
    c = lax.broadcasted_iota(jnp.int32, (tq, tq), 1)
    pairs = [(b, h) for b in range(nb) for h in range(MLA_HEADS)]
    units = range(len(pairs))
    qk_cols = [slice(h * MLA_QK_PAD, (h + 1) * MLA_QK_PAD) for _, h in pairs]
    v_cols = [slice(h * MLA_V, (h + 1) * MLA_V) for _, h in pairs]

    def kv_block(j, masked):
        rows = pl.ds(pl.multiple_of(j * tq, tq), tq)
        s = [_dot_nt(q_ref[b, :, qk_cols[u]], k_ref[b, rows, qk_cols[u]]) for u, (b, _) in enumerate(pairs)]
        if masked:
            s = [jnp.where(c <= r, s[u], -1e30) for u in units]
        m_prev = [m_ref[u] for u in units]
        m_new = [jnp.maximum(m_prev[u], jnp.max(s[u], axis=-1, keepdims=True)) for u in units]
        p = [jnp.exp2(s[u] - jnp.concatenate([m_new[u]] * (tq // LANES), axis=1)) for u in units]
        alpha = [jnp.exp2(m_prev[u] - m_new[u]) for u in units]
        pv = [_dot(p[u].astype(BF16), v_ref[b, rows, v_cols[u]]) for u, (b, _) in enumerate(pairs)]
        for u in units:
            p_lanes = sum(p[u][:, t * LANES:(t + 1) * LANES] for t in range(1, tq // LANES)) + p[u][:, :LANES]
            l_ref[u] = alpha[u] * l_ref[u] + p_lanes
            acc_ref[u] = alpha[u] * acc_ref[u] + pv[u]
            m_ref[u] = m_new[u]

    def kv_step(j, carry):
        kv_block(j, False)
        return carry

    lax.fori_loop(0, i, kv_step, 0)
    kv_block(i, True)
    for u, (b, _) in enumerate(pairs):
        l = jnp.sum(l_ref[u], axis=-1, keepdims=True)
        o_ref[b, :, v_cols[u]] = (acc_ref[u] / l).astype(o_ref.dtype)


def _attention(q, k, v, batch, seq):
    tq = min(ATT_TQ, seq)
    nb = ATT_NB if batch % ATT_NB == 0 else 1
    qk_w, v_w = MLA_HEADS * MLA_QK_PAD, MLA_HEADS * MLA_V
    stat = pltpu.VMEM((nb * MLA_HEADS, tq, LANES), F32)
    out = pl.pallas_call(
        _attention_kernel,
        grid=(batch // nb, seq // tq),
        in_specs=[pl.BlockSpec((nb, tq, qk_w), lambda b, i: (b, i, 0)),
                  pl.BlockSpec((nb, seq, qk_w), lambda b, i: (b, 0, 0)),
                  pl.BlockSpec((nb, seq, v_w), lambda b, i: (b, 0, 0))],
        out_specs=pl.BlockSpec((nb, tq, v_w), lambda b, i: (b, i, 0)),
        out_shape=jax.ShapeDtypeStruct((batch, seq, v_w), BF16),
        scratch_shapes=[stat, stat, pltpu.VMEM((nb * MLA_HEADS, tq, MLA_V), F32)],
        compiler_params=_params(("parallel", "parallel")),
        name="mla_attention",
    )(q.reshape(batch, seq, qk_w), k.reshape(batch, seq, qk_w), v.reshape(batch, seq, v_w))
    return out.reshape(batch * seq, v_w)


MERGE_TM = 512
MERGE_FC = 256
CV_HALO = 32
CV_RB = 64


def _merge_kernel(x_ref, odn_ref, omla_ref,
                  wglu_ref, glub_ref, dww_ref, dwb_ref, lng_ref, lnb_ref,
                  wg_dn_ref, wg_cv_ref, wg_mla_ref, bgate_ref, wdn_ref, wcv_ref, bcv_ref, wmla_ref, wout_ref,
                  g_ref, b_ref, o_ref,
                  hbuf_ref, shifted_ref, cbuf_ref, pdn_ref, pmla_ref, gcv_ref, merged_ref):
    tt = x_ref.shape[0]

    @pl.when(pl.program_id(1) == 0)
    def _():
        hbuf_ref[0:CV_HALO, :] = jnp.zeros((CV_HALO, CV_CHANNELS), F32)

    x = x_ref[...]
    xb = x.astype(BF16)
    u = _dot(xb, wglu_ref[...]) + glub_ref[...]
    hbuf_ref[CV_HALO:CV_HALO + tt, :] = u[:, :CV_CHANNELS] * _sigmoid(u[:, CV_CHANNELS:])

    span = CV_HALO + tt - SUBLANES
    for s in range(1, SUBLANES):
        shifted_ref[s - 1] = hbuf_ref[s:s + span, :]

    for c in range(D_MODEL // MERGE_FC):
        sl = slice(c * MERGE_FC, (c + 1) * MERGE_FC)
        gsl = [slice(i * D_MODEL + c * MERGE_FC, i * D_MODEL + (c + 1) * MERGE_FC) for i in range(N_BRANCHES)]
        g_dn = _sigmoid(_dot(xb, wg_dn_ref[:, sl]) + bgate_ref[:, gsl[0]])
        pdn_ref[:, sl] = g_dn * _dot(odn_ref[...], wdn_ref[:, sl])
        g_mla = _sigmoid(_dot(xb, wg_mla_ref[:, sl]) + bgate_ref[:, gsl[2]])
        pmla_ref[:, sl] = g_mla * _dot(omla_ref[...], wmla_ref[:, sl])
        gcv_ref[:, sl] = _sigmoid(_dot(xb, wg_cv_ref[:, sl]) + bgate_ref[:, gsl[1]])

    base = CV_HALO - (CV_KERNEL - 1)
    for rb in range(tt // CV_RB):
        for s in range(CV_CHANNELS // LANES):
            csl = slice(s * LANES, (s + 1) * LANES)
            acc = jnp.zeros((CV_RB, LANES), F32)
            for j in range(CV_KERNEL):
                shift = (base + j) % SUBLANES
                r0 = rb * CV_RB + base + j - shift
                src = hbuf_ref if shift == 0 else shifted_ref.at[shift - 1]
                acc = acc + dww_ref[j:j + 1, csl] * src[r0:r0 + CV_RB, csl]
            cbuf_ref[rb * CV_RB:(rb + 1) * CV_RB, csl] = acc
    tail = hbuf_ref[tt:tt + CV_HALO, :]
    hbuf_ref[0:CV_HALO, :] = tail

    hcv = _silu(_layer_norm(cbuf_ref[...] + dwb_ref[...], lng_ref[...], lnb_ref[...])).astype(BF16)
    for c in range(D_MODEL // MERGE_FC):
        sl = slice(c * MERGE_FC, (c + 1) * MERGE_FC)
        y_cv = _dot(hcv, wcv_ref[:, sl]) + bcv_ref[:, sl]
        merged_ref[:, sl] = (pdn_ref[:, sl] + gcv_ref[:, sl] * y_cv + pmla_ref[:, sl]).astype(BF16)
    for r0 in range(0, tt, FFN_RB):
        rows = slice(r0, min(r0 + FFN_RB, tt))
        y = DEEPNORM_ALPHA * x[rows] + _dot(merged_ref[rows, :], wout_ref[...])
        o_ref[rows, :] = _layer_norm(y, g_ref[...], b_ref[...])


def _merge(x, odn, omla, batch, seq, consts):
    n = x.shape[0]
    tt = min(MERGE_TM, seq)
    nt = seq // tt
    row = lambda w: pl.BlockSpec((tt, w), lambda b, t: (b * nt + t, 0))
    wide = pltpu.VMEM((tt, D_MODEL), F32)
    return pl.pallas_call(
        _merge_kernel,
        grid=(batch, nt),
        in_specs=[row(D_MODEL), row(DN_DIM), row(MLA_HEADS * MLA_V)] + [spec for _, spec in consts],
        out_specs=row(D_MODEL),
        out_shape=jax.ShapeDtypeStruct((n, D_MODEL), F32),
        scratch_shapes=[pltpu.VMEM((CV_HALO + tt, CV_CHANNELS), F32),
                        pltpu.VMEM((SUBLANES - 1, CV_HALO + tt - SUBLANES, CV_CHANNELS), F32),
                        pltpu.VMEM((tt, CV_CHANNELS), F32),
                        wide, wide, wide,
                        pltpu.VMEM((tt, D_MODEL), BF16)],
        compiler_params=_params(("parallel", "arbitrary")),
        name="merge",
    )(x, odn, omla, *[a for a, _ in consts])


_IN_SIZES = (DN_DIM, DN_DIM, DN_DIM, DN_DIM, DN_HEADS, DN_HEADS, 2 * CV_CHANNELS,
             MLA_Q_LORA, MLA_KV_LORA, MLA_ROPE, N_BRANCHES * D_MODEL)


_IN_OFFS = tuple(int(v) for v in np.concatenate([[0], np.cumsum(_IN_SIZES)]))
_DN_COLS = _IN_OFFS[4] + LANES


def _pad_lanes(v, width):
    return jnp.pad(v.astype(F32), ((0, 0), (0, width - v.shape[1])))


def _split_w_in(w_in):
    part = lambda i, j=None: w_in[:, :, _IN_OFFS[i]:_IN_OFFS[(i if j is None else j) + 1]]
    pad = jnp.zeros(w_in.shape[:2] + (LANES - MLA_ROPE,), w_in.dtype)
    return dict(
        dn=w_in[:, :, :_DN_COLS].astype(BF16),
        latent=jnp.concatenate([part(7, 9), pad], axis=2).astype(BF16),
        wide=jnp.concatenate([part(6), part(10)], axis=2).astype(BF16),
    )


def _pad_w_uq(w_uq):
    depth = w_uq.shape[0]
    w = w_uq.reshape(depth, MLA_Q_LORA, MLA_HEADS, MLA_NOPE + MLA_ROPE)
    w = jnp.pad(w, ((0, 0), (0, 0), (0, 0), (0, MLA_QK_PAD - MLA_NOPE - MLA_ROPE)))
    return w.reshape(depth, MLA_Q_LORA, MLA_HEADS * MLA_QK_PAD).astype(BF16)


def kernel(x, positions, ln1_g, ln1_b, ffn1_w_gate, ffn1_w_up, ffn1_w_down, w_in, b_gate, dn_conv_w, dn_a_log, dn_dt_bias, dn_norm_w, dn_w_o, cv_glu_b, cv_dw_w, cv_dw_b, cv_ln_g, cv_ln_b, cv_w_pw2, cv_b_pw2, mla_q_norm_w, mla_w_uq, mla_kv_norm_w, mla_w_ukv, mla_w_o, w_out, ln2_g, ln2_b, ffn2_w_gate, ffn2_w_up, ffn2_w_down, ln3_g, ln3_b):
    batch, seq, d = x.shape
    n = batch * seq
    h = x.reshape(n, d)

    half = MLA_ROPE // 2
    inv_freq = ROPE_THETA ** (-np.arange(half, dtype=np.float32) / half)
    freq = np.zeros((1, LANES), np.float32)
    freq[0, :half] = inv_freq
    freq[0, half:MLA_ROPE] = inv_freq
    rope_c, rope_s1, rope_s2 = _rope_tables(positions.astype(F32).reshape(n, 1), jnp.asarray(freq))

    bf = lambda a: a.astype(BF16)
    w = _split_w_in(w_in)
    ffn1 = (bf(ffn1_w_gate), bf(ffn1_w_up), bf(ffn1_w_down))
    ffn2 = (bf(ffn2_w_gate), bf(ffn2_w_up), bf(ffn2_w_down))
    dn_w_o_b, cv_w_pw2_b, mla_w_o_b, w_out_b = bf(dn_w_o), bf(cv_w_pw2), bf(mla_w_o), bf(w_out)
    w_uq_b, w_ukv_b = _pad_w_uq(mla_w_uq), bf(mla_w_ukv)
    a_log_p, dt_bias_p = _pad_lanes(dn_a_log, LANES), _pad_lanes(dn_dt_bias, LANES)
    dn_conv_w, cv_dw_w = dn_conv_w.astype(F32), cv_dw_w.astype(F32)

    for l in range(DEPTH):
        h = _ffn_ln(h, [_layer(a, l) for a in ffn1] + [_vec(ln1_g, l), _vec(ln1_b, l)])
        o_dn = _deltanet(h, batch, seq, [
            _layer(w["dn"], l, 3 * DN_DIM, 0), _layer(w["dn"], l, DN_DIM, 3),
            _layer(w["dn"], l, LANES, _IN_OFFS[4] // LANES), _layer(dn_conv_w, l),
            _vec(a_log_p, l), _vec(dt_bias_p, l), _vec(dn_norm_w, l)])
        q, k, v = _mla_prep(h, [
            _layer(w["latent"], l), _vec(mla_q_norm_w, l), _vec(mla_kv_norm_w, l),
            _layer(w_uq_b, l), _layer(w_ukv_b, l)], rope_c, rope_s1, rope_s2)
        o_mla = _attention(q, k, v, batch, seq)
        h = _merge(h, o_dn, o_mla, batch, seq, [
            _layer(w["wide"], l, 2 * CV_CHANNELS, 0), _vec(cv_glu_b, l), _layer(cv_dw_w, l),
            _vec(cv_dw_b, l), _vec(cv_ln_g, l), _vec(cv_ln_b, l),
            _layer(w["wide"], l, D_MODEL, 1), _layer(w["wide"], l, D_MODEL, 2), _layer(w["wide"], l, D_MODEL, 3),
            _vec(b_gate, l), _layer(dn_w_o_b, l), _layer(cv_w_pw2_b, l), _vec(cv_b_pw2, l),
            _layer(mla_w_o_b, l), _layer(w_out_b, l), _vec(ln2_g, l), _vec(ln2_b, l)])
        h = _ffn_ln(h, [_layer(a, l) for a in ffn2] + [_vec(ln3_g, l), _vec(ln3_b, l)])
    return h.reshape(batch, seq, d)
```

```python
import math

import jax
import jax.numpy as jnp
import numpy as np
from jax import lax
from jax.experimental import pallas as pl
from jax.experimental.pallas import tpu as pltpu

D_MODEL = 1024
DEPTH = 2
D_FF = 2816
DN_HEADS = 4
DN_HEAD_DIM = 128
DN_DIM = DN_HEADS * DN_HEAD_DIM
DN_CONV = 4
CV_CHANNELS = 512
CV_KERNEL = 31
MLA_HEADS = 4
MLA_Q_LORA = 384
MLA_KV_LORA = 256
MLA_NOPE = 128
MLA_ROPE = 64
MLA_V = 128
ROPE_THETA = 10000.0
N_BRANCHES = 3
DEEPNORM_ALPHA = (2 * DEPTH) ** 0.25
NORM_EPS = 1e-5

LANES = 128
SUBLANES = 8
MLA_QK_PAD = 256
VMEM_LIMIT = 56 * 1024 * 1024

F32 = jnp.float32
BF16 = jnp.bfloat16
HIGHEST = lax.Precision.HIGHEST


def _dot(a, b, precision=None):
    return jnp.dot(a, b, preferred_element_type=F32, precision=precision)


def _dot_nt(a, b, precision=None):
    return lax.dot_general(a, b, (((1,), (1,)), ((), ())), preferred_element_type=F32, precision=precision)


def _dot_tn(a, b):
    return lax.dot_general(a, b, (((0,), (0,)), ((), ())), preferred_element_type=F32)


def _sigmoid(x):
    return 0.5 * jnp.tanh(0.5 * x) + 0.5


def _silu(x):
    return x * _sigmoid(x)


def _layer_norm(y, g, b):
    mu = jnp.mean(y, axis=-1, keepdims=True)
    d = y - mu
    var = jnp.mean(d * d, axis=-1, keepdims=True)
    return d * lax.rsqrt(var + NORM_EPS) * g + b


def _resident(arr):
    return arr, pl.BlockSpec(arr.shape, lambda *_: (0, 0), pipeline_mode=pl.Buffered(1))


def _layer(arr, l, width=None, col=0):
    width = arr.shape[2] if width is None else width
    return arr, pl.BlockSpec((None, arr.shape[1], width), lambda *_: (l, 0, col), pipeline_mode=pl.Buffered(1))


def _vec(arr, l):
    return _layer(arr.reshape(arr.shape[0], 1, arr.shape[1]), l)


def _params(sem):
    return pltpu.CompilerParams(dimension_semantics=sem, vmem_limit_bytes=VMEM_LIMIT)


FFN_TM = 1024
FFN_FC = 256
FFN_RB = 256


def _ffn_ln_kernel(x_ref, wg_ref, wu_ref, wd_ref, g_ref, b_ref, o_ref, h_ref):
    x = x_ref[...]
    xb = x.astype(BF16)
    for c in range(D_FF // FFN_FC):
        sl = slice(c * FFN_FC, (c + 1) * FFN_FC)
        gate = _dot(xb, wg_ref[:, sl])
        up = _dot(xb, wu_ref[:, sl])
        h_ref[:, sl] = (_silu(gate) * up).astype(BF16)
    for r0 in range(0, x.shape[0], FFN_RB):
        rows = slice(r0, min(r0 + FFN_RB, x.shape[0]))
        y = DEEPNORM_ALPHA * x[rows] + 0.5 * _dot(h_ref[rows, :], wd_ref[...])
        o_ref[rows, :] = _layer_norm(y, g_ref[...], b_ref[...])


def _ffn_ln(x, consts):
    n = x.shape[0]
    tm = min(FFN_TM, n)
    row = pl.BlockSpec((tm, D_MODEL), lambda i: (i, 0))
    return pl.pallas_call(
        _ffn_ln_kernel,
        grid=(n // tm,),
        in_specs=[row] + [spec for _, spec in consts],
        out_specs=row,
        out_shape=jax.ShapeDtypeStruct((n, D_MODEL), F32),
        scratch_shapes=[pltpu.VMEM((tm, D_FF), BF16)],
        compiler_params=_params(("parallel",)),
        name="ffn_ln",
    )(x, *[a for a, _ in consts])


DN_TT = 256
DN_SUB = 64
DN_HALO = SUBLANES
DN_NB = 2


def _deltanet_kernel(x_ref, wqkv_ref, wz_ref, wab_ref, cw_ref, alog_ref, dtb_ref, nw_ref, o_ref,
                     buf_ref, s_ref):
    nb, tt = x_ref.shape[0], x_ref.shape[1]
    seqs = range(nb)

    @pl.when(pl.program_id(1) == 0)
    def _():
        buf_ref[:, 0:DN_HALO, :] = jnp.zeros((nb, DN_HALO, 3 * DN_DIM), F32)
        s_ref[...] = jnp.zeros(s_ref.shape, F32)

    xb = [x_ref[b].astype(BF16) for b in seqs]
    for b in seqs:
        buf_ref[b, DN_HALO:DN_HALO + tt, :] = _dot(xb[b], wqkv_ref[...])
    z = [_dot(xb[b], wz_ref[...]) for b in seqs]
    ab = [_dot(xb[b], wab_ref[...]) for b in seqs]

    qkv = []
    for b in seqs:
        conv = jnp.zeros((tt, 3 * DN_DIM), F32)
        for j in range(DN_CONV):
            off = DN_HALO - (DN_CONV - 1) + j
            conv = conv + cw_ref[j:j + 1, :] * buf_ref[b, off:off + tt, :]
        tail = buf_ref[b, tt:tt + DN_HALO, :]
        buf_ref[b, 0:DN_HALO, :] = tail
        qkv.append(_silu(conv))

    row = lax.broadcasted_iota(jnp.int32, (tt, tt), 0)
    col = lax.broadcasted_iota(jnp.int32, (tt, tt), 1)
    causal = col <= row
    strict = col < row
    same_sub = (row // DN_SUB) == (col // DN_SUB)
    eye = (col == row).astype(F32)
    tril = causal.astype(F32)
    head_lane = lax.broadcasted_iota(jnp.int32, (tt, LANES), 1) < DN_HEADS

    betas, gcs, gcs_t = [], [], []
    for b in seqs:
        sp_in = ab[b] + dtb_ref[...]
        softplus = jnp.maximum(sp_in, 0.0) + jnp.log(1.0 + jnp.exp(-jnp.abs(sp_in)))
        g = jnp.where(head_lane, -jnp.exp(alog_ref[...]) * softplus, 0.0)
        betas.append(_sigmoid(ab[b]))
        gcs.append(_dot(tril, g, HIGHEST))
        gcs_t.append(gcs[b].T)

    pairs = [(b, h) for b in seqs for h in range(DN_HEADS)]
    heads = range(len(pairs))
    hsl = [slice(h * DN_HEAD_DIM, (h + 1) * DN_HEAD_DIM) for _, h in pairs]
    scale = DN_HEAD_DIM ** -0.5

    def unit(x):
        return x * lax.rsqrt(jnp.sum(x * x, axis=-1, keepdims=True) + 1e-6)

    q = [unit(qkv[b][:, h * DN_HEAD_DIM:(h + 1) * DN_HEAD_DIM]) * scale for b, h in pairs]
    k = [unit(qkv[b][:, DN_DIM + h * DN_HEAD_DIM:DN_DIM + (h + 1) * DN_HEAD_DIM]) for b, h in pairs]
    v = [qkv[b][:, 2 * DN_DIM + h * DN_HEAD_DIM:2 * DN_DIM + (h + 1) * DN_HEAD_DIM] for b, h in pairs]
    gc = [gcs[b][:, h:h + 1] for b, h in pairs]
    beta = [betas[b][:, DN_HEADS + h:DN_HEADS + h + 1] for b, h in pairs]
    decay = [jnp.where(causal, jnp.exp(jnp.minimum(gc[u] - gcs_t[b][h:h + 1, :], 0.0)), 0.0)
             for u, (b, h) in enumerate(pairs)]
    kb = [k[h].astype(BF16) for h in heads]
    kk = [_dot_nt(kb[h], kb[h]) for h in heads]
    qk = [_dot_nt(q[h].astype(BF16), kb[h]) for h in heads]
    lmat = [jnp.where(strict, beta[h] * kk[h] * decay[h], 0.0) for h in heads]
    l_diag = [jnp.where(same_sub, lmat[h], 0.0) for h in heads]
    inv_d = [eye - l_diag[h] for h in heads]
    power = [l_diag[h].astype(BF16) for h in heads]
    for _ in range(int(math.log2(DN_SUB)) - 1):
        power = [_dot(power[h], power[h]).astype(BF16) for h in heads]
        inv_d = [inv_d[h] + _dot(inv_d[h].astype(BF16), power[h]) for h in heads]
    inv_db = [inv_d[h].astype(BF16) for h in heads]
    nmat = [_dot(inv_db[h], (lmat[h] - l_diag[h]).astype(BF16)) for h in heads]
    inv_n = [eye - nmat[h] for h in heads]
    power = [nmat[h].astype(BF16) for h in heads]
    for _ in range(int(math.log2(tt // DN_SUB)) - 1):
        power = [_dot(power[h], power[h]).astype(BF16) for h in heads]
        inv_n = [inv_n[h] + _dot(inv_n[h].astype(BF16), power[h]) for h in heads]
    eg = [jnp.exp(gc[h]) for h in heads]
    rhs = [jnp.concatenate([v[h] * beta[h], k[h] * (beta[h] * eg[h])], axis=1).astype(BF16) for h in heads]
    half = [_dot(inv_db[h], rhs[h]).astype(BF16) for h in heads]
    sol = [_dot(inv_n[h].astype(BF16), half[h]) for h in heads]
    qkm = [jnp.where(causal, qk[h] * decay[h], 0.0).astype(BF16) for h in heads]
    g_last = [gc[h][tt - 1:tt, :] for h in heads]
    q_dec = [(q[h] * eg[h]).astype(BF16) for h in heads]
    k_dec = [(k[h] * jnp.exp(g_last[h] - gc[h])).astype(BF16) for h in heads]
    state = [s_ref[h] for h in heads]
    sb = [state[h].astype(BF16) for h in heads]
    vb = [(sol[h][:, :DN_HEAD_DIM] - _dot(sol[h][:, DN_HEAD_DIM:].astype(BF16), sb[h])).astype(BF16)
          for h in heads]
    o = [_dot(q_dec[h], sb[h]) + _dot(qkm[h], vb[h]) for h in heads]
    for h in heads:
        s_ref[h] = state[h] * jnp.exp(g_last[h]) + _dot_tn(k_dec[h], vb[h])
    for u, (b, _) in enumerate(pairs):
        on = o[u] * lax.rsqrt(jnp.mean(o[u] * o[u], axis=-1, keepdims=True) + NORM_EPS) * nw_ref[...]
        o_ref[b, :, hsl[u]] = (on * _silu(z[b][:, hsl[u]])).astype(o_ref.dtype)


def _deltanet(x, batch, seq, consts):
    tt = min(DN_TT, seq)
    nb = DN_NB if batch % DN_NB == 0 else 1
    block = lambda w: pl.BlockSpec((nb, tt, w), lambda i, t: (i, t, 0))
    out = pl.pallas_call(
        _deltanet_kernel,
        grid=(batch // nb, seq // tt),
        in_specs=[block(D_MODEL)] + [spec for _, spec in consts],
        out_specs=block(DN_DIM),
        out_shape=jax.ShapeDtypeStruct((batch, seq, DN_DIM), BF16),
        scratch_shapes=[
            pltpu.VMEM((nb, DN_HALO + tt, 3 * DN_DIM), F32),
            pltpu.VMEM((nb * DN_HEADS, DN_HEAD_DIM, DN_HEAD_DIM), F32),
        ],
        compiler_params=_params(("parallel", "arbitrary")),
        name="deltanet",
    )(x.reshape(batch, seq, D_MODEL), *[a for a, _ in consts])
    return out.reshape(batch * seq, DN_DIM)


ROPE_TM = 512
MLA_TM = 512
ATT_TQ = 256
ATT_NB = 2
MLA_Q_SCALE = (MLA_NOPE + MLA_ROPE) ** -0.5 * math.log2(math.e)


def _rope_table_kernel(pos_ref, freq_ref, c_ref, s1_ref, s2_ref):
    half = MLA_ROPE // 2
    ang = pos_ref[...] * freq_ref[...]
    lane = lax.broadcasted_iota(jnp.int32, ang.shape, 1)
    cos = jnp.cos(ang)
    sin = jnp.sin(ang)
    c_ref[...] = jnp.where(lane < MLA_ROPE, cos, 0.0)
    s1_ref[...] = jnp.where((lane >= half) & (lane < MLA_ROPE), sin, 0.0)
    s2_ref[...] = jnp.where(lane < half, -sin, 0.0)


def _rope_tables(pos, freq):
    n = pos.shape[0]
    tm = min(ROPE_TM, n)
    out = pl.BlockSpec((tm, LANES), lambda i: (i, 0))
    return pl.pallas_call(
        _rope_table_kernel,
        grid=(n // tm,),
        in_specs=[pl.BlockSpec((tm, 1), lambda i: (i, 0)), _resident(freq)[1]],
        out_specs=[out, out, out],
        out_shape=[jax.ShapeDtypeStruct((n, LANES), F32)] * 3,
        compiler_params=_params(("parallel",)),
        name="rope_tables",
    )(pos, freq)


def _rope(x, c, s1, s2):
    half = MLA_ROPE // 2
    return x * c + pltpu.roll(x, half, 1) * s1 + pltpu.roll(x, LANES - half, 1) * s2


def _mla_prep_kernel(x_ref, wlat_ref, qnw_ref, kvnw_ref, wuq_ref, wukv_ref,
                     c_ref, s1_ref, s2_ref, q_out, k_out, v_out):
    lat = _dot(x_ref[...].astype(BF16), wlat_ref[...])
    cq = lat[:, :MLA_Q_LORA]
    ckv = lat[:, MLA_Q_LORA:MLA_Q_LORA + MLA_KV_LORA]
    kr = lat[:, MLA_Q_LORA + MLA_KV_LORA:]
    cqn = cq * lax.rsqrt(jnp.mean(cq * cq, axis=-1, keepdims=True) + NORM_EPS) * qnw_ref[...]
    ckvn = ckv * lax.rsqrt(jnp.mean(ckv * ckv, axis=-1, keepdims=True) + NORM_EPS) * kvnw_ref[...]
    q = _dot(cqn.astype(BF16), wuq_ref[...]) * MLA_Q_SCALE
    kv = _dot(ckvn.astype(BF16), wukv_ref[...])
    c, s1, s2 = c_ref[...], s1_ref[...], s2_ref[...]
    kr_rot = _rope(kr, c, s1, s2).astype(BF16)
    for h in range(MLA_HEADS):
        b0 = h * MLA_QK_PAD
        q_out[:, b0:b0 + LANES] = q[:, b0:b0 + LANES].astype(BF16)
        q_out[:, b0 + LANES:b0 + 2 * LANES] = _rope(q[:, b0 + LANES:b0 + 2 * LANES], c, s1, s2).astype(BF16)
        k_out[:, b0:b0 + LANES] = kv[:, b0:b0 + LANES].astype(BF16)
        k_out[:, b0 + LANES:b0 + 2 * LANES] = kr_rot
        v_out[:, h * MLA_V:(h + 1) * MLA_V] = kv[:, b0 + LANES:b0 + 2 * LANES].astype(BF16)


def _mla_prep(x, consts, c, s1, s2):
    n = x.shape[0]
    tm = min(MLA_TM, n)
    row = lambda w: pl.BlockSpec((tm, w), lambda i: (i, 0))
    return pl.pallas_call(
        _mla_prep_kernel,
        grid=(n // tm,),
        in_specs=[row(D_MODEL)] + [spec for _, spec in consts] + [row(LANES)] * 3,
        out_specs=[row(MLA_HEADS * MLA_QK_PAD), row(MLA_HEADS * MLA_QK_PAD), row(MLA_HEADS * MLA_V)],
        out_shape=[jax.ShapeDtypeStruct((n, MLA_HEADS * MLA_QK_PAD), BF16),
                   jax.ShapeDtypeStruct((n, MLA_HEADS * MLA_QK_PAD), BF16),
                   jax.ShapeDtypeStruct((n, MLA_HEADS * MLA_V), BF16)],
        compiler_params=_params(("parallel",)),
        name="mla_prep",
    )(x, *[a for a, _ in consts], c, s1, s2)


def _attention_kernel(q_ref, k_ref, v_ref, o_ref, m_ref, l_ref, acc_ref):
    nb, tq = q_ref.shape[0], q_ref.shape[1]
    i = pl.program_id(1)
    m_ref[...] = jnp.full(m_ref.shape, -1e30, F32)
    l_ref[...] = jnp.zeros(l_ref.shape, F32)
    acc_ref[...] = jnp.zeros(acc_ref.shape, F32)
    r = lax.broadcasted_iota(jnp.int32, (tq, tq), 0)
    c = lax.broadcasted_iota(jnp.int32, (tq, tq), 1)
    pairs = [(b, h) for b in range(nb) for h in range(MLA_HEADS)]
    units = range(len(pairs))
    qk_cols = [slice(h * MLA_QK_PAD, (h + 1) * MLA_QK_PAD) for _, h in pairs]
    v_cols = [slice(h * MLA_V, (h + 1) * MLA_V) for _, h in pairs]

    def kv_block(j, masked):
        rows = pl.ds(pl.multiple_of(j * tq, tq), tq)
        s = [_dot_nt(q_ref[b, :, qk_cols[u]], k_ref[b, rows, qk_cols[u]]) for u, (b, _) in enumerate(pairs)]
        if masked:
            s = [jnp.where(c <= r, s[u], -1e30) for u in units]
        m_prev = [m_ref[u] for u in units]
        m_new = [jnp.maximum(m_prev[u], jnp.max(s[u], axis=-1, keepdims=True)) for u in units]
        p = [jnp.exp2(s[u] - jnp.concatenate([m_new[u]] * (tq // LANES), axis=1)) for u in units]
        alpha = [jnp.exp2(m_prev[u] - m_new[u]) for u in units]
        pv = [_dot(p[u].astype(BF16), v_ref[b, rows, v_cols[u]]) for u, (b, _) in enumerate(pairs)]
        for u in units:
            p_lanes = sum(p[u][:, t * LANES:(t + 1) * LANES] for t in range(1, tq // LANES)) + p[u][:, :LANES]
            l_ref[u] = alpha[u] * l_ref[u] + p_lanes
            acc_ref[u] = alpha[u] * acc_ref[u] + pv[u]
            m_ref[u] = m_new[u]

    def kv_step(j, carry):
        kv_block(j, False)
        return carry

    lax.fori_loop(0, i, kv_step, 0)
    kv_block(i, True)
    for u, (b, _) in enumerate(pairs):
        l = jnp.sum(l_ref[u], axis=-1, keepdims=True)
        o_ref[b, :, v_cols[u]] = (acc_ref[u] / l).astype(o_ref.dtype)


def _attention(q, k, v, batch, seq):
    tq = min(ATT_TQ, seq)
    nb = ATT_NB if batch % ATT_NB == 0 else 1
    qk_w, v_w = MLA_HEADS * MLA_QK_PAD, MLA_HEADS * MLA_V
    stat = pltpu.VMEM((nb * MLA_HEADS, tq, LANES), F32)
    out = pl.pallas_call(
        _attention_kernel,
        grid=(batch // nb, seq // tq),
        in_specs=[pl.BlockSpec((nb, tq, qk_w), lambda b, i: (b, i, 0)),
                  pl.BlockSpec((nb, seq, qk_w), lambda b, i: (b, 0, 0)),
                  pl.BlockSpec((nb, seq, v_w), lambda b, i: (b, 0, 0))],
        out_specs=pl.BlockSpec((nb, tq, v_w), lambda b, i: (b, i, 0)),
        out_shape=jax.ShapeDtypeStruct((batch, seq, v_w), BF16),
        scratch_shapes=[stat, stat, pltpu.VMEM((nb * MLA_HEADS, tq, MLA_V), F32)],
        compiler_params=_params(("parallel", "parallel")),
        name="mla_attention",
    )(q.reshape(batch, seq, qk_w), k.reshape(batch, seq, qk_w), v.reshape(batch, seq, v_w))
    return out.reshape(batch * seq, v_w)


MERGE_TM = 512
MERGE_FC = 256
CV_HALO = 32
CV_RB = 64


def _merge_kernel(x_ref, odn_ref, omla_ref,
                  wglu_ref, glub_ref, dww_ref, dwb_ref, lng_ref, lnb_ref,
                  wg_dn_ref, wg_cv_ref, wg_mla_ref, bgate_ref, wdn_ref, wcv_ref, bcv_ref, wmla_ref, wout_ref,
                  g_ref, b_ref, o_ref,
                  hbuf_ref, shifted_ref, cbuf_ref, pdn_ref, pmla_ref, gcv_ref, merged_ref):
    tt = x_ref.shape[0]

    @pl.when(pl.program_id(1) == 0)
    def _():
        hbuf_ref[0:CV_HALO, :] = jnp.zeros((CV_HALO, CV_CHANNELS), F32)

    x = x_ref[...]
    xb = x.astype(BF16)
    u = _dot(xb, wglu_ref[...]) + glub_ref[...]
    hbuf_ref[CV_HALO:CV_HALO + tt, :] = u[:, :CV_CHANNELS] * _sigmoid(u[:, CV_CHANNELS:])

    span = CV_HALO + tt - SUBLANES
    for s in range(1, SUBLANES):
        shifted_ref[s - 1] = hbuf_ref[s:s + span, :]

    for c in range(D_MODEL // MERGE_FC):
        sl = slice(c * MERGE_FC, (c + 1) * MERGE_FC)
        gsl = [slice(i * D_MODEL + c * MERGE_FC, i * D_MODEL + (c + 1) * MERGE_FC) for i in range(N_BRANCHES)]
        g_dn = _sigmoid(_dot(xb, wg_dn_ref[:, sl]) + bgate_ref[:, gsl[0]])
        pdn_ref[:, sl] = g_dn * _dot(odn_ref[...], wdn_ref[:, sl])
        g_mla = _sigmoid(_dot(xb, wg_mla_ref[:, sl]) + bgate_ref[:, gsl[2]])
        pmla_ref[:, sl] = g_mla * _dot(omla_ref[...], wmla_ref[:, sl])
        gcv_ref[:, sl] = _sigmoid(_dot(xb, wg_cv_ref[:, sl]) + bgate_ref[:, gsl[1]])

    base = CV_HALO - (CV_KERNEL - 1)
    for rb in range(tt // CV_RB):
        for s in range(CV_CHANNELS // LANES):
            csl = slice(s * LANES, (s + 1) * LANES)
            acc = jnp.zeros((CV_RB, LANES), F32)
            for j in range(CV_KERNEL):
                shift = (base + j) % SUBLANES
                r0 = rb * CV_RB + base + j - shift
                src = hbuf_ref if shift == 0 else shifted_ref.at[shift - 1]
                acc = acc + dww_ref[j:j + 1, csl] * src[r0:r0 + CV_RB, csl]
            cbuf_ref[rb * CV_RB:(rb + 1) * CV_RB, csl] = acc
    tail = hbuf_ref[tt:tt + CV_HALO, :]
    hbuf_ref[0:CV_HALO, :] = tail

    hcv = _silu(_layer_norm(cbuf_ref[...] + dwb_ref[...], lng_ref[...], lnb_ref[...])).astype(BF16)
    for c in range(D_MODEL // MERGE_FC):
        sl = slice(c * MERGE_FC, (c + 1) * MERGE_FC)
        y_cv = _dot(hcv, wcv_ref[:, sl]) + bcv_ref[:, sl]
        merged_ref[:, sl] = (pdn_ref[:, sl] + gcv_ref[:, sl] * y_cv + pmla_ref[:, sl]).astype(BF16)
    for r0 in range(0, tt, FFN_RB):
        rows = slice(r0, min(r0 + FFN_RB, tt))
        y = DEEPNORM_ALPHA * x[rows] + _dot(merged_ref[rows, :], wout_ref[...])
        o_ref[rows, :] = _layer_norm(y, g_ref[...], b_ref[...])


def _merge(x, odn, omla, batch, seq, consts):
    n = x.shape[0]
    tt = min(MERGE_TM, seq)
    nt = seq // tt
    row = lambda w: pl.BlockSpec((tt, w), lambda b, t: (b * nt + t, 0))
    wide = pltpu.VMEM((tt, D_MODEL), F32)
    return pl.pallas_call(
        _merge_kernel,
        grid=(batch, nt),
        in_specs=[row(D_MODEL), row(DN_DIM), row(MLA_HEADS * MLA_V)] + [spec for _, spec in consts],
        out_specs=row(D_MODEL),
        out_shape=jax.ShapeDtypeStruct((n, D_MODEL), F32),
        scratch_shapes=[pltpu.VMEM((CV_HALO + tt, CV_CHANNELS), F32),
                        pltpu.VMEM((SUBLANES - 1, CV_HALO + tt - SUBLANES, CV_CHANNELS), F32),
                        pltpu.VMEM((tt, CV_CHANNELS), F32),
                        wide, wide, wide,
                        pltpu.VMEM((tt, D_MODEL), BF16)],
        compiler_params=_params(("parallel", "arbitrary")),
        name="merge",
    )(x, odn, omla, *[a for a, _ in consts])


_IN_SIZES = (DN_DIM, DN_DIM, DN_DIM, DN_DIM, DN_HEADS, DN_HEADS, 2 * CV_CHANNELS,
             MLA_Q_LORA, MLA_KV_LORA, MLA_ROPE, N_BRANCHES * D_MODEL)


_IN_OFFS = tuple(int(v) for v in np.concatenate([[0], np.cumsum(_IN_SIZES)]))
_DN_COLS = _IN_OFFS[4] + LANES


def _pad_lanes(v, width):
    return jnp.pad(v.astype(F32), ((0, 0), (0, width - v.shape[1])))


def _split_w_in(w_in):
    part = lambda i, j=None: w_in[:, :, _IN_OFFS[i]:_IN_OFFS[(i if j is None else j) + 1]]
    pad = jnp.zeros(w_in.shape[:2] + (LANES - MLA_ROPE,), w_in.dtype)
    return dict(
        dn=w_in[:, :, :_DN_COLS].astype(BF16),
        latent=jnp.concatenate([part(7, 9), pad], axis=2).astype(BF16),
        wide=jnp.concatenate([part(6), part(10)], axis=2).astype(BF16),
    )


def _pad_w_uq(w_uq):
    depth = w_uq.shape[0]
    w = w_uq.reshape(depth, MLA_Q_LORA, MLA_HEADS, MLA_NOPE + MLA_ROPE)
    w = jnp.pad(w, ((0, 0), (0, 0), (0, 0), (0, MLA_QK_PAD - MLA_NOPE - MLA_ROPE)))
    return w.reshape(depth, MLA_Q_LORA, MLA_HEADS * MLA_QK_PAD).astype(BF16)


def kernel(x, positions, ln1_g, ln1_b, ffn1_w_gate, ffn1_w_up, ffn1_w_down, w_in, b_gate, dn_conv_w, dn_a_log, dn_dt_bias, dn_norm_w, dn_w_o, cv_glu_b, cv_dw_w, cv_dw_b, cv_ln_g, cv_ln_b, cv_w_pw2, cv_b_pw2, mla_q_norm_w, mla_w_uq, mla_kv_norm_w, mla_w_ukv, mla_w_o, w_out, ln2_g, ln2_b, ffn2_w_gate, ffn2_w_up, ffn2_w_down, ln3_g, ln3_b):
    batch, seq, d = x.shape
    n = batch * seq
    h = x.reshape(n, d)

    half = MLA_ROPE // 2
    inv_freq = ROPE_THETA ** (-np.arange(half, dtype=np.float32) / half)
    freq = np.zeros((1, LANES), np.float32)
    freq[0, :half] = inv_freq
    freq[0, half:MLA_ROPE] = inv_freq
    rope_c, rope_s1, rope_s2 = _rope_tables(positions.astype(F32).reshape(n, 1), jnp.asarray(freq))

    bf = lambda a: a.astype(BF16)
    w = _split_w_in(w_in)
    ffn1 = (bf(ffn1_w_gate), bf(ffn1_w_up), bf(ffn1_w_down))
    ffn2 = (bf(ffn2_w_gate), bf(ffn2_w_up), bf(ffn2_w_down))
    dn_w_o_b, cv_w_pw2_b, mla_w_o_b, w_out_b = bf(dn_w_o), bf(cv_w_pw2), bf(mla_w_o), bf(w_out)
    w_uq_b, w_ukv_b = _pad_w_uq(mla_w_uq), bf(mla_w_ukv)
    a_log_p, dt_bias_p = _pad_lanes(dn_a_log, LANES), _pad_lanes(dn_dt_bias, LANES)
    dn_conv_w, cv_dw_w = dn_conv_w.astype(F32), cv_dw_w.astype(F32)

    for l in range(DEPTH):
        h = _ffn_ln(h, [_layer(a, l) for a in ffn1] + [_vec(ln1_g, l), _vec(ln1_b, l)])
        o_dn = _deltanet(h, batch, seq, [
            _layer(w["dn"], l, 3 * DN_DIM, 0), _layer(w["dn"], l, DN_DIM, 3),
            _layer(w["dn"], l, LANES, _IN_OFFS[4] // LANES), _layer(dn_conv_w, l),
            _vec(a_log_p, l), _vec(dt_bias_p, l), _vec(dn_norm_w, l)])
        q, k, v = _mla_prep(h, [
            _layer(w["latent"], l), _vec(mla_q_norm_w, l), _vec(mla_kv_norm_w, l),
            _layer(w_uq_b, l), _layer(w_ukv_b, l)], rope_c, rope_s1, rope_s2)
        o_mla = _attention(q, k, v, batch, seq)
        h = _merge(h, o_dn, o_mla, batch, seq, [
            _layer(w["wide"], l, 2 * CV_CHANNELS, 0), _vec(cv_glu_b, l), _layer(cv_dw_w, l),
            _vec(cv_dw_b, l), _vec(cv_ln_g, l), _vec(cv_ln_b, l),
            _layer(w["wide"], l, D_MODEL, 1), _layer(w["wide"], l, D_MODEL, 2), _layer(w["wide"], l, D_MODEL, 3),
            _vec(b_gate, l), _layer(dn_w_o_b, l), _layer(cv_w_pw2_b, l), _vec(cv_b_pw2, l),
            _layer(mla_w_o_b, l), _layer(w_out_b, l), _vec(ln2_g, l), _vec(ln2_b, l)])
        h = _ffn_ln(h, [_layer(a, l) for a in ffn2] + [_vec(ln3_g, l), _vec(ln3_b, l)])
    return h.reshape(batch, seq, d)
```

```python
import math

import jax
import jax.numpy as jnp
import numpy as np
from jax import lax
from jax.experimental import pallas as pl
from jax.experimental.pallas import tpu as pltpu

D_MODEL = 1024
DEPTH = 2
D_FF = 2816
DN_HEADS = 4
DN_HEAD_DIM = 128
DN_DIM = DN_HEADS * DN_HEAD_DIM
DN_CONV = 4
CV_CHANNELS = 512
CV_KERNEL = 31
MLA_HEADS = 4
MLA_Q_LORA = 384
MLA_KV_LORA = 256
MLA_NOPE = 128
MLA_ROPE = 64
MLA_V = 128
ROPE_THETA = 10000.0
N_BRANCHES = 3
DEEPNORM_ALPHA = (2 * DEPTH) ** 0.25
NORM_EPS = 1e-5

LANES = 128
SUBLANES = 8
MLA_QK_PAD = 256
VMEM_LIMIT = 56 * 1024 * 1024

F32 = jnp.float32
BF16 = jnp.bfloat16
HIGHEST = lax.Precision.HIGHEST


def _dot(a, b, precision=None):
    return jnp.dot(a, b, preferred_element_type=F32, precision=precision)


def _dot_nt(a, b, precision=None):
    return lax.dot_general(a, b, (((1,), (1,)), ((), ())), preferred_element_type=F32, precision=precision)


def _dot_tn(a, b):
    return lax.dot_general(a, b, (((0,), (0,)), ((), ())), preferred_element_type=F32)


def _sigmoid(x):
    return 0.5 * jnp.tanh(0.5 * x) + 0.5


def _silu(x):
    return x * _sigmoid(x)


def _layer_norm(y, g, b):
    mu = jnp.mean(y, axis=-1, keepdims=True)
    d = y - mu
    var = jnp.mean(d * d, axis=-1, keepdims=True)
    return d * lax.rsqrt(var + NORM_EPS) * g + b


def _resident(arr):
    return arr, pl.BlockSpec(arr.shape, lambda *_: (0, 0), pipeline_mode=pl.Buffered(1))


def _layer(arr, l, width=None, col=0):
    width = arr.shape[2] if width is None else width
    return arr, pl.BlockSpec((None, arr.shape[1], width), lambda *_: (l, 0, col), pipeline_mode=pl.Buffered(1))


def _vec(arr, l):
    return _layer(arr.reshape(arr.shape[0], 1, arr.shape[1]), l)


def _params(sem):
    return pltpu.CompilerParams(dimension_semantics=sem, vmem_limit_bytes=VMEM_LIMIT)


FFN_TM = 1024
FFN_FC = 256
FFN_RB = 256


def _ffn_ln_kernel(x_ref, wg_ref, wu_ref, wd_ref, g_ref, b_ref, o_ref, h_ref):
    x = x_ref[...]
    xb = x.astype(BF16)
    for c in range(D_FF // FFN_FC):
        sl = slice(c * FFN_FC, (c + 1) * FFN_FC)
        gate = _dot(xb, wg_ref[:, sl])
        up = _dot(xb, wu_ref[:, sl])
        h_ref[:, sl] = (_silu(gate) * up).astype(BF16)
    for r0 in range(0, x.shape[0], FFN_RB):
        rows = slice(r0, min(r0 + FFN_RB, x.shape[0]))
        y = DEEPNORM_ALPHA * x[rows] + 0.5 * _dot(h_ref[rows, :], wd_ref[...])
        o_ref[rows, :] = _layer_norm(y, g_ref[...], b_ref[...])


def _ffn_ln(x, consts):
    n = x.shape[0]
    tm = min(FFN_TM, n)
    row = pl.BlockSpec((tm, D_MODEL), lambda i: (i, 0))
    return pl.pallas_call(
        _ffn_ln_kernel,
        grid=(n // tm,),
        in_specs=[row] + [spec for _, spec in consts],
        out_specs=row,
        out_shape=jax.ShapeDtypeStruct((n, D_MODEL), F32),
        scratch_shapes=[pltpu.VMEM((tm, D_FF), BF16)],
        compiler_params=_params(("parallel",)),
        name="ffn_ln",
    )(x, *[a for a, _ in consts])


DN_TT = 256
DN_SUB = 64
DN_HALO = SUBLANES
DN_NB = 2


def _deltanet_kernel(x_ref, wqkv_ref, wz_ref, wab_ref, cw_ref, alog_ref, dtb_ref, nw_ref, o_ref,
                     buf_ref, s_ref):
    nb, tt = x_ref.shape[0], x_ref.shape[1]
    seqs = range(nb)

    @pl.when(pl.program_id(1) == 0)
    def _():
        buf_ref[:, 0:DN_HALO, :] = jnp.zeros((nb, DN_HALO, 3 * DN_DIM), F32)
        s_ref[...] = jnp.zeros(s_ref.shape, F32)

    xb = [x_ref[b].astype(BF16) for b in seqs]
    for b in seqs:
        buf_ref[b, DN_HALO:DN_HALO + tt, :] = _dot(xb[b], wqkv_ref[...])
    z = [_dot(xb[b], wz_ref[...]) for b in seqs]
    ab = [_dot(xb[b], wab_ref[...]) for b in seqs]

    qkv = []
    for b in seqs:
        conv = jnp.zeros((tt, 3 * DN_DIM), F32)
        for j in range(DN_CONV):
            off = DN_HALO - (DN_CONV - 1) + j
            conv = conv + cw_ref[j:j + 1, :] * buf_ref[b, off:off + tt, :]
        tail = buf_ref[b, tt:tt + DN_HALO, :]
        buf_ref[b, 0:DN_HALO, :] = tail
        qkv.append(_silu(conv))

    row = lax.broadcasted_iota(jnp.int32, (tt, tt), 0)
    col = lax.broadcasted_iota(jnp.int32, (tt, tt), 1)
    causal = col <= row
    strict = col < row
    same_sub = (row // DN_SUB) == (col // DN_SUB)
    eye = (col == row).astype(F32)
    tril = causal.astype(F32)
    head_lane = lax.broadcasted_iota(jnp.int32, (tt, LANES), 1) < DN_HEADS

    betas, gcs, gcs_t = [], [], []
    for b in seqs:
        sp_in = ab[b] + dtb_ref[...]
        softplus = jnp.maximum(sp_in, 0.0) + jnp.log(1.0 + jnp.exp(-jnp.abs(sp_in)))
        g = jnp.where(head_lane, -jnp.exp(alog_ref[...]) * softplus, 0.0)
        betas.append(_sigmoid(ab[b]))
        gcs.append(_dot(tril, g, HIGHEST))
        gcs_t.append(gcs[b].T)

    pairs = [(b, h) for b in seqs for h in range(DN_HEADS)]
    heads = range(len(pairs))
    hsl = [slice(h * DN_HEAD_DIM, (h + 1) * DN_HEAD_DIM) for _, h in pairs]
    scale = DN_HEAD_DIM ** -0.5

    def unit(x):
        return x * lax.rsqrt(jnp.sum(x * x, axis=-1, keepdims=True) + 1e-6)

    q = [unit(qkv[b][:, h * DN_HEAD_DIM:(h + 1) * DN_HEAD_DIM]) * scale for b, h in pairs]
    k = [unit(qkv[b][:, DN_DIM + h * DN_HEAD_DIM:DN_DIM + (h + 1) * DN_HEAD_DIM]) for b, h in pairs]
    v = [qkv[b][:, 2 * DN_DIM + h * DN_HEAD_DIM:2 * DN_DIM + (h + 1) * DN_HEAD_DIM] for b, h in pairs]
    gc = [gcs[b][:, h:h + 1] for b, h in pairs]
    beta = [betas[b][:, DN_HEADS + h:DN_HEADS + h + 1] for b, h in pairs]
    decay = [jnp.where(causal, jnp.exp(jnp.minimum(gc[u] - gcs_t[b][h:h + 1, :], 0.0)), 0.0)
             for u, (b, h) in enumerate(pairs)]
    kb = [k[h].astype(BF16) for h in heads]
    kk = [_dot_nt(kb[h], kb[h]) for h in heads]
    qk = [_dot_nt(q[h].astype(BF16), kb[h]) for h in heads]
    lmat = [jnp.where(strict, beta[h] * kk[h] * decay[h], 0.0) for h in heads]
    l_diag = [jnp.where(same_sub, lmat[h], 0.0) for h in heads]
    inv_d = [eye - l_diag[h] for h in heads]
    power = [l_diag[h].astype(BF16) for h in heads]
    for _ in range(int(math.log2(DN_SUB)) - 1):
        power = [_dot(power[h], power[h]).astype(BF16) for h in heads]
        inv_d = [inv_d[h] + _dot(inv_d[h].astype(BF16), power[h]) for h in heads]
    inv_db = [inv_d[h].astype(BF16) for h in heads]
    nmat = [_dot(inv_db[h], (lmat[h] - l_diag[h]).astype(BF16)) for h in heads]
    inv_n = [eye - nmat[h] for h in heads]
    power = [nmat[h].astype(BF16) for h in heads]
    for _ in range(int(math.log2(tt // DN_SUB)) - 1):
        power = [_dot(power[h], power[h]).astype(BF16) for h in heads]
        inv_n = [inv_n[h] + _dot(inv_n[h].astype(BF16), power[h]) for h in heads]
    eg = [jnp.exp(gc[h]) for h in heads]
    rhs = [jnp.concatenate([v[h] * beta[h], k[h] * (beta[h] * eg[h])], axis=1).astype(BF16) for h in heads]
    half = [_dot(inv_db[h], rhs[h]).astype(BF16) for h in heads]
    sol = [_dot(inv_n[h].astype(BF16), half[h]) for h in heads]
    qkm = [jnp.where(causal, qk[h] * decay[h], 0.0).astype(BF16) for h in heads]
    g_last = [gc[h][tt - 1:tt, :] for h in heads]
    q_dec = [(q[h] * eg[h]).astype(BF16) for h in heads]
    k_dec = [(k[h] * jnp.exp(g_last[h] - gc[h])).astype(BF16) for h in heads]
    state = [s_ref[h] for h in heads]
    sb = [state[h].astype(BF16) for h in heads]
    vb = [(sol[h][:, :DN_HEAD_DIM] - _dot(sol[h][:, DN_HEAD_DIM:].astype(BF16), sb[h])).astype(BF16)
          for h in heads]
    o = [_dot(q_dec[h], sb[h]) + _dot(qkm[h], vb[h]) for h in heads]
    for h in heads:
        s_ref[h] = state[h] * jnp.exp(g_last[h]) + _dot_tn(k_dec[h], vb[h])
    for u, (b, _) in enumerate(pairs):
        on = o[u] * lax.rsqrt(jnp.mean(o[u] * o[u], axis=-1, keepdims=True) + NORM_EPS) * nw_ref[...]
        o_ref[b, :, hsl[u]] = (on * _silu(z[b][:, hsl[u]])).astype(o_ref.dtype)


def _deltanet(x, batch, seq, consts):
    tt = min(DN_TT, seq)
    nb = DN_NB if batch % DN_NB == 0 else 1
    block = lambda w: pl.BlockSpec((nb, tt, w), lambda i, t: (i, t, 0))
    out = pl.pallas_call(
        _deltanet_kernel,
        grid=(batch // nb, seq // tt),
        in_specs=[block(D_MODEL)] + [spec for _, spec in consts],
        out_specs=block(DN_DIM),
        out_shape=jax.ShapeDtypeStruct((batch, seq, DN_DIM), BF16),
        scratch_shapes=[
            pltpu.VMEM((nb, DN_HALO + tt, 3 * DN_DIM), F32),
            pltpu.VMEM((nb * DN_HEADS, DN_HEAD_DIM, DN_HEAD_DIM), F32),
        ],
        compiler_params=_params(("parallel", "arbitrary")),
        name="deltanet",
    )(x.reshape(batch, seq, D_MODEL), *[a for a, _ in consts])
    return out.reshape(batch * seq, DN_DIM)


ROPE_TM = 512
MLA_TM = 1024
ATT_TQ = 256
ATT_NB = 2
MLA_Q_SCALE = (MLA_NOPE + MLA_ROPE) ** -0.5 * math.log2(math.e)


def _rope_table_kernel(pos_ref, freq_ref, c_ref, s1_ref, s2_ref):
    half = MLA_ROPE // 2
    ang = pos_ref[...] * freq_ref[...]
    lane = lax.broadcasted_iota(jnp.int32, ang.shape, 1)
    cos = jnp.cos(ang)
    sin = jnp.sin(ang)
    c_ref[...] = jnp.where(lane < MLA_ROPE, cos, 0.0)
    s1_ref[...] = jnp.where((lane >= half) & (lane < MLA_ROPE), sin, 0.0)
    s2_ref[...] = jnp.where(lane < half, -sin, 0.0)


def _rope_tables(pos, freq):
    n = pos.shape[0]
    tm = min(ROPE_TM, n)
    out = pl.BlockSpec((tm, LANES), lambda i: (i, 0))
    return pl.pallas_call(
        _rope_table_kernel,
        grid=(n // tm,),
        in_specs=[pl.BlockSpec((tm, 1), lambda i: (i, 0)), _resident(freq)[1]],
        out_specs=[out, out, out],
        out_shape=[jax.ShapeDtypeStruct((n, LANES), F32)] * 3,
        compiler_params=_params(("parallel",)),
        name="rope_tables",
    )(pos, freq)


def _rope(x, c, s1, s2):
    half = MLA_ROPE // 2
    return x * c + pltpu.roll(x, half, 1) * s1 + pltpu.roll(x, LANES - half, 1) * s2


def _mla_prep_kernel(x_ref, wlat_ref, qnw_ref, kvnw_ref, wuq_ref, wukv_ref,
                     c_ref, s1_ref, s2_ref, q_out, k_out, v_out):
    lat = _dot(x_ref[...].astype(BF16), wlat_ref[...])
    cq = lat[:, :MLA_Q_LORA]
    ckv = lat[:, MLA_Q_LORA:MLA_Q_LORA + MLA_KV_LORA]
    kr = lat[:, MLA_Q_LORA + MLA_KV_LORA:]
    cqn = cq * lax.rsqrt(jnp.mean(cq * cq, axis=-1, keepdims=True) + NORM_EPS) * qnw_ref[...]
    ckvn = ckv * lax.rsqrt(jnp.mean(ckv * ckv, axis=-1, keepdims=True) + NORM_EPS) * kvnw_ref[...]
    q = _dot(cqn.astype(BF16), wuq_ref[...]) * MLA_Q_SCALE
    kv = _dot(ckvn.astype(BF16), wukv_ref[...])
    c, s1, s2 = c_ref[...], s1_ref[...], s2_ref[...]
    kr_rot = _rope(kr, c, s1, s2).astype(BF16)
    for h in range(MLA_HEADS):
        b0 = h * MLA_QK_PAD
        q_out[:, b0:b0 + LANES] = q[:, b0:b0 + LANES].astype(BF16)
        q_out[:, b0 + LANES:b0 + 2 * LANES] = _rope(q[:, b0 + LANES:b0 + 2 * LANES], c, s1, s2).astype(BF16)
        k_out[:, b0:b0 + LANES] = kv[:, b0:b0 + LANES].astype(BF16)
        k_out[:, b0 + LANES:b0 + 2 * LANES] = kr_rot
        v_out[:, h * MLA_V:(h + 1) * MLA_V] = kv[:, b0 + LANES:b0 + 2 * LANES].astype(BF16)


def _mla_prep(x, consts, c, s1, s2):
    n = x.shape[0]
    tm = min(MLA_TM, n)
    row = lambda w: pl.BlockSpec((tm, w), lambda i: (i, 0))
    return pl.pallas_call(
        _mla_prep_kernel,
        grid=(n // tm,),
        in_specs=[row(D_MODEL)] + [spec for _, spec in consts] + [row(LANES)] * 3,
        out_specs=[row(MLA_HEADS * MLA_QK_PAD), row(MLA_HEADS * MLA_QK_PAD), row(MLA_HEADS * MLA_V)],
        out_shape=[jax.ShapeDtypeStruct((n, MLA_HEADS * MLA_QK_PAD), BF16),
                   jax.ShapeDtypeStruct((n, MLA_HEADS * MLA_QK_PAD), BF16),
                   jax.ShapeDtypeStruct((n, MLA_HEADS * MLA_V), BF16)],
        compiler_params=_params(("parallel",)),
        name="mla_prep",
    )(x, *[a for a, _ in consts], c, s1, s2)


def _attention_kernel(q_ref, k_ref, v_ref, o_ref, m_ref, l_ref, acc_ref):
    nb, tq = q_ref.shape[0], q_ref.shape[1]
    i = pl.program_id(1)
    m_ref[...] = jnp.full(m_ref.shape, -1e30, F32)
    l_ref[...] = jnp.zeros(l_ref.shape, F32)
    acc_ref[...] = jnp.zeros(acc_ref.shape, F32)
    r = lax.broadcasted_iota(jnp.int32, (tq, tq), 0)
    c = lax.broadcasted_iota(jnp.int32, (tq, tq), 1)
    pairs = [(b, h) for b in range(nb) for h in range(MLA_HEADS)]
    units = range(len(pairs))
    qk_cols = [slice(h * MLA_QK_PAD, (h + 1) * MLA_QK_PAD) for _, h in pairs]
    v_cols = [slice(h * MLA_V, (h + 1) * MLA_V) for _, h in pairs]

    def kv_block(j, masked):
        rows = pl.ds(pl.multiple_of(j * tq, tq), tq)
        s = [_dot_nt(q_ref[b, :, qk_cols[u]], k_ref[b, rows, qk_cols[u]]) for u, (b, _) in enumerate(pairs)]
        if masked:
            s = [jnp.where(c <= r, s[u], -1e30) for u in units]
        m_prev = [m_ref[u] for u in units]
        m_new = [jnp.maximum(m_prev[u], jnp.max(s[u], axis=-1, keepdims=True)) for u in units]
        p = [jnp.exp2(s[u] - jnp.concatenate([m_new[u]] * (tq // LANES), axis=1)) for u in units]
        alpha = [jnp.exp2(m_prev[u] - m_new[u]) for u in units]
        pv = [_dot(p[u].astype(BF16), v_ref[b, rows, v_cols[u]]) for u, (b, _) in enumerate(pairs)]
        for u in units:
            p_lanes = sum(p[u][:, t * LANES:(t + 1) * LANES] for t in range(1, tq // LANES)) + p[u][:, :LANES]
            l_ref[u] = alpha[u] * l_ref[u] + p_lanes
            acc_ref[u] = alpha[u] * acc_ref[u] + pv[u]
            m_ref[u] = m_new[u]

    def kv_step(j, carry):
        kv_block(j, False)
        return carry

    lax.fori_loop(0, i, kv_step, 0)
    kv_block(i, True)
    for u, (b, _) in enumerate(pairs):
        l = jnp.sum(l_ref[u], axis=-1, keepdims=True)
        o_ref[b, :, v_cols[u]] = (acc_ref[u] / l).astype(o_ref.dtype)


def _attention(q, k, v, batch, seq):
    tq = min(ATT_TQ, seq)
    nb = ATT_NB if batch % ATT_NB == 0 else 1
    qk_w, v_w = MLA_HEADS * MLA_QK_PAD, MLA_HEADS * MLA_V
    stat = pltpu.VMEM((nb * MLA_HEADS, tq, LANES), F32)
    out = pl.pallas_call(
        _attention_kernel,
        grid=(batch // nb, seq // tq),
        in_specs=[pl.BlockSpec((nb, tq, qk_w), lambda b, i: (b, i, 0)),
                  pl.BlockSpec((nb, seq, qk_w), lambda b, i: (b, 0, 0)),
                  pl.BlockSpec((nb, seq, v_w), lambda b, i: (b, 0, 0))],
        out_specs=pl.BlockSpec((nb, tq, v_w), lambda b, i: (b, i, 0)),
        out_shape=jax.ShapeDtypeStruct((batch, seq, v_w), BF16),
        scratch_shapes=[stat, stat, pltpu.VMEM((nb * MLA_HEADS, tq, MLA_V), F32)],
        compiler_params=_params(("parallel", "parallel")),
        name="mla_attention",
    )(q.reshape(batch, seq, qk_w), k.reshape(batch, seq, qk_w), v.reshape(batch, seq, v_w))
    return out.reshape(batch * seq, v_w)


MERGE_TM = 512
MERGE_FC = 256
CV_HALO = 32
CV_RB = 64


def _merge_kernel(x_ref, odn_ref, omla_ref,
                  wglu_ref, glub_ref, dww_ref, dwb_ref, lng_ref, lnb_ref,
                  wg_dn_ref, wg_cv_ref, wg_mla_ref, bgate_ref, wdn_ref, wcv_ref, bcv_ref, wmla_ref, wout_ref,
                  g_ref, b_ref, o_ref,
                  hbuf_ref, shifted_ref, cbuf_ref, pdn_ref, pmla_ref, gcv_ref, merged_ref):
    tt = x_ref.shape[0]

    @pl.when(pl.program_id(1) == 0)
    def _():
        hbuf_ref[0:CV_HALO, :] = jnp.zeros((CV_HALO, CV_CHANNELS), F32)

    x = x_ref[...]
    xb = x.astype(BF16)
    u = _dot(xb, wglu_ref[...]) + glub_ref[...]
    hbuf_ref[CV_HALO:CV_HALO + tt, :] = u[:, :CV_CHANNELS] * _sigmoid(u[:, CV_CHANNELS:])

    span = CV_HALO + tt - SUBLANES
    for s in range(1, SUBLANES):
        shifted_ref[s - 1] = hbuf_ref[s:s + span, :]

    for c in range(D_MODEL // MERGE_FC):
        sl = slice(c * MERGE_FC, (c + 1) * MERGE_FC)
        gsl = [slice(i * D_MODEL + c * MERGE_FC, i * D_MODEL + (c + 1) * MERGE_FC) for i in range(N_BRANCHES)]
        g_dn = _sigmoid(_dot(xb, wg_dn_ref[:, sl]) + bgate_ref[:, gsl[0]])
        pdn_ref[:, sl] = g_dn * _dot(odn_ref[...], wdn_ref[:, sl])
        g_mla = _sigmoid(_dot(xb, wg_mla_ref[:, sl]) + bgate_ref[:, gsl[2]])
        pmla_ref[:, sl] = g_mla * _dot(omla_ref[...], wmla_ref[:, sl])
        gcv_ref[:, sl] = _sigmoid(_dot(xb, wg_cv_ref[:, sl]) + bgate_ref[:, gsl[1]])

    base = CV_HALO - (CV_KERNEL - 1)
    for rb in range(tt // CV_RB):
        for s in range(CV_CHANNELS // LANES):
            csl = slice(s * LANES, (s + 1) * LANES)
            acc = jnp.zeros((CV_RB, LANES), F32)
            for j in range(CV_KERNEL):
                shift = (base + j) % SUBLANES
                r0 = rb * CV_RB + base + j - shift
                src = hbuf_ref if shift == 0 else shifted_ref.at[shift - 1]
                acc = acc + dww_ref[j:j + 1, csl] * src[r0:r0 + CV_RB, csl]
            cbuf_ref[rb * CV_RB:(rb + 1) * CV_RB, csl] = acc
    tail = hbuf_ref[tt:tt + CV_HALO, :]
    hbuf_ref[0:CV_HALO, :] = tail

    hcv = _silu(_layer_norm(cbuf_ref[...] + dwb_ref[...], lng_ref[...], lnb_ref[...])).astype(BF16)
    for c in range(D_MODEL // MERGE_FC):
        sl = slice(c * MERGE_FC, (c + 1) * MERGE_FC)
        y_cv = _dot(hcv, wcv_ref[:, sl]) + bcv_ref[:, sl]
        merged_ref[:, sl] = (pdn_ref[:, sl] + gcv_ref[:, sl] * y_cv + pmla_ref[:, sl]).astype(BF16)
    for r0 in range(0, tt, FFN_RB):
        rows = slice(r0, min(r0 + FFN_RB, tt))
        y = DEEPNORM_ALPHA * x[rows] + _dot(merged_ref[rows, :], wout_ref[...])
        o_ref[rows, :] = _layer_norm(y, g_ref[...], b_ref[...])


def _merge(x, odn, omla, batch, seq, consts):
    n = x.shape[0]
    tt = min(MERGE_TM, seq)
    nt = seq // tt
    row = lambda w: pl.BlockSpec((tt, w), lambda b, t: (b * nt + t, 0))
    wide = pltpu.VMEM((tt, D_MODEL), F32)
    return pl.pallas_call(
        _merge_kernel,
        grid=(batch, nt),
        in_specs=[row(D_MODEL), row(DN_DIM), row(MLA_HEADS * MLA_V)] + [spec for _, spec in consts],
        out_specs=row(D_MODEL),
        out_shape=jax.ShapeDtypeStruct((n, D_MODEL), F32),
        scratch_shapes=[pltpu.VMEM((CV_HALO + tt, CV_CHANNELS), F32),
                        pltpu.VMEM((SUBLANES - 1, CV_HALO + tt - SUBLANES, CV_CHANNELS), F32),
                        pltpu.VMEM((tt, CV_CHANNELS), F32),
                        wide, wide, wide,
                        pltpu.VMEM((tt, D_MODEL), BF16)],
        compiler_params=_params(("parallel", "arbitrary")),
        name="merge",
    )(x, odn, omla, *[a for a, _ in consts])


_IN_SIZES = (DN_DIM, DN_DIM, DN_DIM, DN_DIM, DN_HEADS, DN_HEADS, 2 * CV_CHANNELS,
             MLA_Q_LORA, MLA_KV_LORA, MLA_ROPE, N_BRANCHES * D_MODEL)


_IN_OFFS = tuple(int(v) for v in np.concatenate([[0], np.cumsum(_IN_SIZES)]))
_DN_COLS = _IN_OFFS[4] + LANES


def _pad_lanes(v, width):
    return jnp.pad(v.astype(F32), ((0, 0), (0, width - v.shape[1])))


def _split_w_in(w_in):
    part = lambda i, j=None: w_in[:, :, _IN_OFFS[i]:_IN_OFFS[(i if j is None else j) + 1]]
    pad = jnp.zeros(w_in.shape[:2] + (LANES - MLA_ROPE,), w_in.dtype)
    return dict(
        dn=w_in[:, :, :_DN_COLS].astype(BF16),
        latent=jnp.concatenate([part(7, 9), pad], axis=2).astype(BF16),
        wide=jnp.concatenate([part(6), part(10)], axis=2).astype(BF16),
    )


def _pad_w_uq(w_uq):
    depth = w_uq.shape[0]
    w = w_uq.reshape(depth, MLA_Q_LORA, MLA_HEADS, MLA_NOPE + MLA_ROPE)
    w = jnp.pad(w, ((0, 0), (0, 0), (0, 0), (0, MLA_QK_PAD - MLA_NOPE - MLA_ROPE)))
    return w.reshape(depth, MLA_Q_LORA, MLA_HEADS * MLA_QK_PAD).astype(BF16)


def kernel(x, positions, ln1_g, ln1_b, ffn1_w_gate, ffn1_w_up, ffn1_w_down, w_in, b_gate, dn_conv_w, dn_a_log, dn_dt_bias, dn_norm_w, dn_w_o, cv_glu_b, cv_dw_w, cv_dw_b, cv_ln_g, cv_ln_b, cv_w_pw2, cv_b_pw2, mla_q_norm_w, mla_w_uq, mla_kv_norm_w, mla_w_ukv, mla_w_o, w_out, ln2_g, ln2_b, ffn2_w_gate, ffn2_w_up, ffn2_w_down, ln3_g, ln3_b):
    batch, seq, d = x.shape
    n = batch * seq
    h = x.reshape(n, d)

    half = MLA_ROPE // 2
    inv_freq = ROPE_THETA ** (-np.arange(half, dtype=np.float32) / half)
    freq = np.zeros((1, LANES), np.float32)
    freq[0, :half] = inv_freq
    freq[0, half:MLA_ROPE] = inv_freq
    rope_c, rope_s1, rope_s2 = _rope_tables(positions.astype(F32).reshape(n, 1), jnp.asarray(freq))

    bf = lambda a: a.astype(BF16)
    w = _split_w_in(w_in)
    ffn1 = (bf(ffn1_w_gate), bf(ffn1_w_up), bf(ffn1_w_down))
    ffn2 = (bf(ffn2_w_gate), bf(ffn2_w_up), bf(ffn2_w_down))
    dn_w_o_b, cv_w_pw2_b, mla_w_o_b, w_out_b = bf(dn_w_o), bf(cv_w_pw2), bf(mla_w_o), bf(w_out)
    w_uq_b, w_ukv_b = _pad_w_uq(mla_w_uq), bf(mla_w_ukv)
    a_log_p, dt_bias_p = _pad_lanes(dn_a_log, LANES), _pad_lanes(dn_dt_bias, LANES)
    dn_conv_w, cv_dw_w = dn_conv_w.astype(F32), cv_dw_w.astype(F32)

    for l in range(DEPTH):
        h = _ffn_ln(h, [_layer(a, l) for a in ffn1] + [_vec(ln1_g, l), _vec(ln1_b, l)])
        o_dn = _deltanet(h, batch, seq, [
            _layer(w["dn"], l, 3 * DN_DIM, 0), _layer(w["dn"], l, DN_DIM, 3),
            _layer(w["dn"], l, LANES, _IN_OFFS[4] // LANES), _layer(dn_conv_w, l),
            _vec(a_log_p, l), _vec(dt_bias_p, l), _vec(dn_norm_w, l)])
        q, k, v = _mla_prep(h, [
            _layer(w["latent"], l), _vec(mla_q_norm_w, l), _vec(mla_kv_norm_w, l),
            _layer(w_uq_b, l), _layer(w_ukv_b, l)], rope_c, rope_s1, rope_s2)
        o_mla = _attention(q, k, v, batch, seq)
        h = _merge(h, o_dn, o_mla, batch, seq, [
            _layer(w["wide"], l, 2 * CV_CHANNELS, 0), _vec(cv_glu_b, l), _layer(cv_dw_w, l),
            _vec(cv_dw_b, l), _vec(cv_ln_g, l), _vec(cv_ln_b, l),
            _layer(w["wide"], l, D_MODEL, 1), _layer(w["wide"], l, D_MODEL, 2), _layer(w["wide"], l, D_MODEL, 3),
            _vec(b_gate, l), _layer(dn_w_o_b, l), _layer(cv_w_pw2_b, l), _vec(cv_b_pw2, l),
            _layer(mla_w_o_b, l), _layer(w_out_b, l), _vec(ln2_g, l), _vec(ln2_b, l)])
        h = _ffn_ln(h, [_layer(a, l) for a in ffn2] + [_vec(ln3_g, l), _vec(ln3_b, l)])
    return h.reshape(batch, seq, d)
```

```python
import math

import jax
import jax.numpy as jnp
import numpy as np
from jax import lax
from jax.experimental import pallas as pl
from jax.experimental.pallas import tpu as pltpu

D_MODEL = 1024
DEPTH = 2
D_FF = 2816
DN_HEADS = 4
DN_HEAD_DIM = 128
DN_DIM = DN_HEADS * DN_HEAD_DIM
DN_CONV = 4
CV_CHANNELS = 512
CV_KERNEL = 31
MLA_HEADS = 4
MLA_Q_LORA = 384
MLA_KV_LORA = 256
MLA_NOPE = 128
MLA_ROPE = 64
MLA_V = 128
ROPE_THETA = 10000.0
N_BRANCHES = 3
DEEPNORM_ALPHA = (2 * DEPTH) ** 0.25
NORM_EPS = 1e-5

LANES = 128
SUBLANES = 8
MLA_QK_PAD = 256
VMEM_LIMIT = 56 * 1024 * 1024

F32 = jnp.float32
BF16 = jnp.bfloat16
HIGHEST = lax.Precision.HIGHEST


def _dot(a, b, precision=None):
    return jnp.dot(a, b, preferred_element_type=F32, precision=precision)


def _dot_nt(a, b, precision=None):
    return lax.dot_general(a, b, (((1,), (1,)), ((), ())), preferred_element_type=F32, precision=precision)


def _dot_tn(a, b):
    return lax.dot_general(a, b, (((0,), (0,)), ((), ())), preferred_element_type=F32)


def _sigmoid(x):
    return 0.5 * jnp.tanh(0.5 * x) + 0.5


def _silu(x):
    half = 0.5 * x
    return half * jnp.tanh(half) + half


def _layer_norm(y, g, b):
    mu = jnp.mean(y, axis=-1, keepdims=True)
    d = y - mu
    var = jnp.mean(d * d, axis=-1, keepdims=True)
    return d * lax.rsqrt(var + NORM_EPS) * g + b


def _resident(arr):
    return arr, pl.BlockSpec(arr.shape, lambda *_: (0, 0), pipeline_mode=pl.Buffered(1))


def _layer(arr, l, width=None, col=0):
    width = arr.shape[2] if width is None else width
    return arr, pl.BlockSpec((None, arr.shape[1], width), lambda *_: (l, 0, col), pipeline_mode=pl.Buffered(1))


def _vec(arr, l):
    return _layer(arr.reshape(arr.shape[0], 1, arr.shape[1]), l)


def _params(sem):
    return pltpu.CompilerParams(dimension_semantics=sem, vmem_limit_bytes=VMEM_LIMIT)


FFN_TM = 1024
FFN_FC = 256
FFN_RB = 256


def _ffn_ln_kernel(x_ref, wg_ref, wu_ref, wd_ref, g_ref, b_ref, o_ref, h_ref):
    x = x_ref[...]
    xb = x.astype(BF16)
    for c in range(D_FF // FFN_FC):
        sl = slice(c * FFN_FC, (c + 1) * FFN_FC)
        gate = _dot(xb, wg_ref[:, sl])
        up = _dot(xb, wu_ref[:, sl])
        h_ref[:, sl] = (_silu(gate) * up).astype(BF16)
    for r0 in range(0, x.shape[0], FFN_RB):
        rows = slice(r0, min(r0 + FFN_RB, x.shape[0]))
        y = DEEPNORM_ALPHA * x[rows] + 0.5 * _dot(h_ref[rows, :], wd_ref[...])
        o_ref[rows, :] = _layer_norm(y, g_ref[...], b_ref[...])


def _ffn_ln(x, consts):
    n = x.shape[0]
    tm = min(FFN_TM, n)
    row = pl.BlockSpec((tm, D_MODEL), lambda i: (i, 0))
    return pl.pallas_call(
        _ffn_ln_kernel,
        grid=(n // tm,),
        in_specs=[row] + [spec for _, spec in consts],
        out_specs=row,
        out_shape=jax.ShapeDtypeStruct((n, D_MODEL), F32),
        scratch_shapes=[pltpu.VMEM((tm, D_FF), BF16)],
        compiler_params=_params(("parallel",)),
        name="ffn_ln",
    )(x, *[a for a, _ in consts])


DN_TT = 256
DN_SUB = 64
DN_HALO = SUBLANES
DN_NB = 2


def _deltanet_kernel(x_ref, wqkv_ref, wz_ref, wab_ref, cw_ref, alog_ref, dtb_ref, nw_ref, o_ref,
                     buf_ref, s_ref):
    nb, tt = x_ref.shape[0], x_ref.shape[1]
    seqs = range(nb)

    @pl.when(pl.program_id(1) == 0)
    def _():
        buf_ref[:, 0:DN_HALO, :] = jnp.zeros((nb, DN_HALO, 3 * DN_DIM), F32)
        s_ref[...] = jnp.zeros(s_ref.shape, F32)

    xb = [x_ref[b].astype(BF16) for b in seqs]
    for b in seqs:
        buf_ref[b, DN_HALO:DN_HALO + tt, :] = _dot(xb[b], wqkv_ref[...])
    z = [_dot(xb[b], wz_ref[...]) for b in seqs]
    ab = [_dot(xb[b], wab_ref[...]) for b in seqs]

    qkv = []
    for b in seqs:
        conv = jnp.zeros((tt, 3 * DN_DIM), F32)
        for j in range(DN_CONV):
            off = DN_HALO - (DN_CONV - 1) + j
            conv = conv + cw_ref[j:j + 1, :] * buf_ref[b, off:off + tt, :]
        tail = buf_ref[b, tt:tt + DN_HALO, :]
        buf_ref[b, 0:DN_HALO, :] = tail
        qkv.append(_silu(conv))

    row = lax.broadcasted_iota(jnp.int32, (tt, tt), 0)
    col = lax.broadcasted_iota(jnp.int32, (tt, tt), 1)
    causal = col <= row
    strict = col < row
    same_sub = (row // DN_SUB) == (col // DN_SUB)
    eye = (col == row).astype(F32)
    tril = causal.astype(F32)
    head_lane = lax.broadcasted_iota(jnp.int32, (tt, LANES), 1) < DN_HEADS

    betas, gcs, gcs_t = [], [], []
    for b in seqs:
        sp_in = ab[b] + dtb_ref[...]
        softplus = jnp.maximum(sp_in, 0.0) + jnp.log(1.0 + jnp.exp(-jnp.abs(sp_in)))
        g = jnp.where(head_lane, -jnp.exp(alog_ref[...]) * softplus, 0.0)
        betas.append(_sigmoid(ab[b]))
        gcs.append(_dot(tril, g, HIGHEST))
        gcs_t.append(gcs[b].T)

    pairs = [(b, h) for b in seqs for h in range(DN_HEADS)]
    heads = range(len(pairs))
    hsl = [slice(h * DN_HEAD_DIM, (h + 1) * DN_HEAD_DIM) for _, h in pairs]
    scale = DN_HEAD_DIM ** -0.5

    def unit(x):
        return x * lax.rsqrt(jnp.sum(x * x, axis=-1, keepdims=True) + 1e-6)

    q = [unit(qkv[b][:, h * DN_HEAD_DIM:(h + 1) * DN_HEAD_DIM]) * scale for b, h in pairs]
    k = [unit(qkv[b][:, DN_DIM + h * DN_HEAD_DIM:DN_DIM + (h + 1) * DN_HEAD_DIM]) for b, h in pairs]
    v = [qkv[b][:, 2 * DN_DIM + h * DN_HEAD_DIM:2 * DN_DIM + (h + 1) * DN_HEAD_DIM] for b, h in pairs]
    gc = [gcs[b][:, h:h + 1] for b, h in pairs]
    beta = [betas[b][:, DN_HEADS + h:DN_HEADS + h + 1] for b, h in pairs]
    decay = [jnp.where(causal, jnp.exp(jnp.minimum(gc[u] - gcs_t[b][h:h + 1, :], 0.0)), 0.0)
             for u, (b, h) in enumerate(pairs)]
    kb = [k[h].astype(BF16) for h in heads]
    kk = [_dot_nt(kb[h], kb[h]) for h in heads]
    qk = [_dot_nt(q[h].astype(BF16), kb[h]) for h in heads]
    lmat = [jnp.where(strict, beta[h] * kk[h] * decay[h], 0.0) for h in heads]
    l_diag = [jnp.where(same_sub, lmat[h], 0.0) for h in heads]
    inv_d = [eye - l_diag[h] for h in heads]
    power = [l_diag[h].astype(BF16) for h in heads]
    for _ in range(int(math.log2(DN_SUB)) - 1):
        power = [_dot(power[h], power[h]).astype(BF16) for h in heads]
        inv_d = [inv_d[h] + _dot(inv_d[h].astype(BF16), power[h]) for h in heads]
    inv_db = [inv_d[h].astype(BF16) for h in heads]
    nmat = [_dot(inv_db[h], (lmat[h] - l_diag[h]).astype(BF16)) for h in heads]
    inv_n = [eye - nmat[h] for h in heads]
    power = [nmat[h].astype(BF16) for h in heads]
    for _ in range(int(math.log2(tt // DN_SUB)) - 1):
        power = [_dot(power[h], power[h]).astype(BF16) for h in heads]
        inv_n = [inv_n[h] + _dot(inv_n[h].astype(BF16), power[h]) for h in heads]
    eg = [jnp.exp(gc[h]) for h in heads]
    rhs = [jnp.concatenate([v[h] * beta[h], k[h] * (beta[h] * eg[h])], axis=1).astype(BF16) for h in heads]
    half = [_dot(inv_db[h], rhs[h]).astype(BF16) for h in heads]
    sol = [_dot(inv_n[h].astype(BF16), half[h]) for h in heads]
    qkm = [jnp.where(causal, qk[h] * decay[h], 0.0).astype(BF16) for h in heads]
    g_last = [gc[h][tt - 1:tt, :] for h in heads]
    q_dec = [(q[h] * eg[h]).astype(BF16) for h in heads]
    k_dec = [(k[h] * jnp.exp(g_last[h] - gc[h])).astype(BF16) for h in heads]
    state = [s_ref[h] for h in heads]
    sb = [state[h].astype(BF16) for h in heads]
    vb = [(sol[h][:, :DN_HEAD_DIM] - _dot(sol[h][:, DN_HEAD_DIM:].astype(BF16), sb[h])).astype(BF16)
          for h in heads]
    o = [_dot(q_dec[h], sb[h]) + _dot(qkm[h], vb[h]) for h in heads]
    for h in heads:
        s_ref[h] = state[h] * jnp.exp(g_last[h]) + _dot_tn(k_dec[h], vb[h])
    for u, (b, _) in enumerate(pairs):
        on = o[u] * lax.rsqrt(jnp.mean(o[u] * o[u], axis=-1, keepdims=True) + NORM_EPS) * nw_ref[...]
        o_ref[b, :, hsl[u]] = (on * _silu(z[b][:, hsl[u]])).astype(o_ref.dtype)


def _deltanet(x, batch, seq, consts):
    tt = min(DN_TT, seq)
    nb = DN_NB if batch % DN_NB == 0 else 1
    block = lambda w: pl.BlockSpec((nb, tt, w), lambda i, t: (i, t, 0))
    out = pl.pallas_call(
        _deltanet_kernel,
        grid=(batch // nb, seq // tt),
        in_specs=[block(D_MODEL)] + [spec for _, spec in consts],
        out_specs=block(DN_DIM),
        out_shape=jax.ShapeDtypeStruct((batch, seq, DN_DIM), BF16),
        scratch_shapes=[
            pltpu.VMEM((nb, DN_HALO + tt, 3 * DN_DIM), F32),
            pltpu.VMEM((nb * DN_HEADS, DN_HEAD_DIM, DN_HEAD_DIM), F32),
        ],
        compiler_params=_params(("parallel", "arbitrary")),
        name="deltanet",
    )(x.reshape(batch, seq, D_MODEL), *[a for a, _ in consts])
    return out.reshape(batch * seq, DN_DIM)


ROPE_TM = 512
MLA_TM = 1024
ATT_TQ = 256
ATT_NB = 2
MLA_Q_SCALE = (MLA_NOPE + MLA_ROPE) ** -0.5 * math.log2(math.e)


def _rope_table_kernel(pos_ref, freq_ref, c_ref, s1_ref, s2_ref):
    half = MLA_ROPE // 2
    ang = pos_ref[...] * freq_ref[...]
    lane = lax.broadcasted_iota(jnp.int32, ang.shape, 1)
    cos = jnp.cos(ang)
    sin = jnp.sin(ang)
    c_ref[...] = jnp.where(lane < MLA_ROPE, cos, 0.0)
    s1_ref[...] = jnp.where((lane >= half) & (lane < MLA_ROPE), sin, 0.0)
    s2_ref[...] = jnp.where(lane < half, -sin, 0.0)


def _rope_tables(pos, freq):
    n = pos.shape[0]
    tm = min(ROPE_TM, n)
    out = pl.BlockSpec((tm, LANES), lambda i: (i, 0))
    return pl.pallas_call(
        _rope_table_kernel,
        grid=(n // tm,),
        in_specs=[pl.BlockSpec((tm, 1), lambda i: (i, 0)), _resident(freq)[1]],
        out_specs=[out, out, out],
        out_shape=[jax.ShapeDtypeStruct((n, LANES), F32)] * 3,
        compiler_params=_params(("parallel",)),
        name="rope_tables",
    )(pos, freq)


def _rope(x, c, s1, s2):
    half = MLA_ROPE // 2
    return x * c + pltpu.roll(x, half, 1) * s1 + pltpu.roll(x, LANES - half, 1) * s2


def _mla_prep_kernel(x_ref, wlat_ref, qnw_ref, kvnw_ref, wuq_ref, wukv_ref,
                     c_ref, s1_ref, s2_ref, q_out, k_out, v_out):
    lat = _dot(x_ref[...].astype(BF16), wlat_ref[...])
    cq = lat[:, :MLA_Q_LORA]
    ckv = lat[:, MLA_Q_LORA:MLA_Q_LORA + MLA_KV_LORA]
    kr = lat[:, MLA_Q_LORA + MLA_KV_LORA:]
    cqn = cq * lax.rsqrt(jnp.mean(cq * cq, axis=-1, keepdims=True) + NORM_EPS) * qnw_ref[...]
    ckvn = ckv * lax.rsqrt(jnp.mean(ckv * ckv, axis=-1, keepdims=True) + NORM_EPS) * kvnw_ref[...]
    q = _dot(cqn.astype(BF16), wuq_ref[...]) * MLA_Q_SCALE
    kv = _dot(ckvn.astype(BF16), wukv_ref[...])
    c, s1, s2 = c_ref[...], s1_ref[...], s2_ref[...]
    kr_rot = _rope(kr, c, s1, s2).astype(BF16)
    for h in range(MLA_HEADS):
        b0 = h * MLA_QK_PAD
        q_out[:, b0:b0 + LANES] = q[:, b0:b0 + LANES].astype(BF16)
        q_out[:, b0 + LANES:b0 + 2 * LANES] = _rope(q[:, b0 + LANES:b0 + 2 * LANES], c, s1, s2).astype(BF16)
        k_out[:, b0:b0 + LANES] = kv[:, b0:b0 + LANES].astype(BF16)
        k_out[:, b0 + LANES:b0 + 2 * LANES] = kr_rot
        v_out[:, h * MLA_V:(h + 1) * MLA_V] = kv[:, b0 + LANES:b0 + 2 * LANES].astype(BF16)


def _mla_prep(x, consts, c, s1, s2):
    n = x.shape[0]
    tm = min(MLA_TM, n)
    row = lambda w: pl.BlockSpec((tm, w), lambda i: (i, 0))
    return pl.pallas_call(
        _mla_prep_kernel,
        grid=(n // tm,),
        in_specs=[row(D_MODEL)] + [spec for _, spec in consts] + [row(LANES)] * 3,
        out_specs=[row(MLA_HEADS * MLA_QK_PAD), row(MLA_HEADS * MLA_QK_PAD), row(MLA_HEADS * MLA_V)],
        out_shape=[jax.ShapeDtypeStruct((n, MLA_HEADS * MLA_QK_PAD), BF16),
                   jax.ShapeDtypeStruct((n, MLA_HEADS * MLA_QK_PAD), BF16),
                   jax.ShapeDtypeStruct((n, MLA_HEADS * MLA_V), BF16)],
        compiler_params=_params(("parallel",)),
        name="mla_prep",
    )(x, *[a for a, _ in consts], c, s1, s2)


def _attention_kernel(q_ref, k_ref, v_ref, o_ref, m_ref, l_ref, acc_ref):
    nb, tq = q_ref.shape[0], q_ref.shape[1]
    i = pl.program_id(1)
    m_ref[...] = jnp.full(m_ref.shape, -1e30, F32)
    l_ref[...] = jnp.zeros(l_ref.shape, F32)
    acc_ref[...] = jnp.zeros(acc_ref.shape, F32)
    r = lax.broadcasted_iota(jnp.int32, (tq, tq), 0)
    c = lax.broadcasted_iota(jnp.int32, (tq, tq), 1)
    pairs = [(b, h) for b in range(nb) for h in range(MLA_HEADS)]
    units = range(len(pairs))
    qk_cols = [slice(h * MLA_QK_PAD, (h + 1) * MLA_QK_PAD) for _, h in pairs]
    v_cols = [slice(h * MLA_V, (h + 1) * MLA_V) for _, h in pairs]

    def kv_block(j, masked):
        rows = pl.ds(pl.multiple_of(j * tq, tq), tq)
        s = [_dot_nt(q_ref[b, :, qk_cols[u]], k_ref[b, rows, qk_cols[u]]) for u, (b, _) in enumerate(pairs)]
        if masked:
            s = [jnp.where(c <= r, s[u], -1e30) for u in units]
        m_prev = [m_ref[u] for u in units]
        m_new = [jnp.maximum(m_prev[u], jnp.max(s[u], axis=-1, keepdims=True)) for u in units]
        p = [jnp.exp2(s[u] - jnp.concatenate([m_new[u]] * (tq // LANES), axis=1)) for u in units]
        alpha = [jnp.exp2(m_prev[u] - m_new[u]) for u in units]
        pv = [_dot(p[u].astype(BF16), v_ref[b, rows, v_cols[u]]) for u, (b, _) in enumerate(pairs)]
        for u in units:
            p_lanes = sum(p[u][:, t * LANES:(t + 1) * LANES] for t in range(1, tq // LANES)) + p[u][:, :LANES]
            l_ref[u] = alpha[u] * l_ref[u] + p_lanes
            acc_ref[u] = alpha[u] * acc_ref[u] + pv[u]
            m_ref[u] = m_new[u]

    def kv_step(j, carry):
        kv_block(j, False)
        return carry

    lax.fori_loop(0, i, kv_step, 0)
    kv_block(i, True)
    for u, (b, _) in enumerate(pairs):
        l = jnp.sum(l_ref[u], axis=-1, keepdims=True)
        o_ref[b, :, v_cols[u]] = (acc_ref[u] / l).astype(o_ref.dtype)


def _attention(q, k, v, batch, seq):
    tq = min(ATT_TQ, seq)
    nb = ATT_NB if batch % ATT_NB == 0 else 1
    qk_w, v_w = MLA_HEADS * MLA_QK_PAD, MLA_HEADS * MLA_V
    stat = pltpu.VMEM((nb * MLA_HEADS, tq, LANES), F32)
    out = pl.pallas_call(
        _attention_kernel,
        grid=(batch // nb, seq // tq),
        in_specs=[pl.BlockSpec((nb, tq, qk_w), lambda b, i: (b, i, 0)),
                  pl.BlockSpec((nb, seq, qk_w), lambda b, i: (b, 0, 0)),
                  pl.BlockSpec((nb, seq, v_w), lambda b, i: (b, 0, 0))],
        out_specs=pl.BlockSpec((nb, tq, v_w), lambda b, i: (b, i, 0)),
        out_shape=jax.ShapeDtypeStruct((batch, seq, v_w), BF16),
        scratch_shapes=[stat, stat, pltpu.VMEM((nb * MLA_HEADS, tq, MLA_V), F32)],
        compiler_params=_params(("parallel", "parallel")),
        name="mla_attention",
    )(q.reshape(batch, seq, qk_w), k.reshape(batch, seq, qk_w), v.reshape(batch, seq, v_w))
    return out.reshape(batch * seq, v_w)


MERGE_TM = 512
MERGE_FC = 256
CV_HALO = 32
CV_RB = 64


def _merge_kernel(x_ref, odn_ref, omla_ref,
                  wglu_ref, glub_ref, dww_ref, dwb_ref, lng_ref, lnb_ref,
                  wg_dn_ref, wg_cv_ref, wg_mla_ref, bgate_ref, wdn_ref, wcv_ref, bcv_ref, wmla_ref, wout_ref,
                  g_ref, b_ref, o_ref,
                  hbuf_ref, shifted_ref, cbuf_ref, pdn_ref, pmla_ref, gcv_ref, merged_ref):
    tt = x_ref.shape[0]

    @pl.when(pl.program_id(1) == 0)
    def _():
        hbuf_ref[0:CV_HALO, :] = jnp.zeros((CV_HALO, CV_CHANNELS), F32)

    x = x_ref[...]
    xb = x.astype(BF16)
    u = _dot(xb, wglu_ref[...]) + glub_ref[...]
    hbuf_ref[CV_HALO:CV_HALO + tt, :] = u[:, :CV_CHANNELS] * _sigmoid(u[:, CV_CHANNELS:])

    span = CV_HALO + tt - SUBLANES
    for s in range(1, SUBLANES):
        shifted_ref[s - 1] = hbuf_ref[s:s + span, :]

    for c in range(D_MODEL // MERGE_FC):
        sl = slice(c * MERGE_FC, (c + 1) * MERGE_FC)
        gsl = [slice(i * D_MODEL + c * MERGE_FC, i * D_MODEL + (c + 1) * MERGE_FC) for i in range(N_BRANCHES)]
        g_dn = _sigmoid(_dot(xb, wg_dn_ref[:, sl]) + bgate_ref[:, gsl[0]])
        pdn_ref[:, sl] = g_dn * _dot(odn_ref[...], wdn_ref[:, sl])
        g_mla = _sigmoid(_dot(xb, wg_mla_ref[:, sl]) + bgate_ref[:, gsl[2]])
        pmla_ref[:, sl] = g_mla * _dot(omla_ref[...], wmla_ref[:, sl])
        gcv_ref[:, sl] = _sigmoid(_dot(xb, wg_cv_ref[:, sl]) + bgate_ref[:, gsl[1]])

    base = CV_HALO - (CV_KERNEL - 1)
    for rb in range(tt // CV_RB):
        for s in range(CV_CHANNELS // LANES):
            csl = slice(s * LANES, (s + 1) * LANES)
            acc = jnp.zeros((CV_RB, LANES), F32)
            for j in range(CV_KERNEL):
                shift = (base + j) % SUBLANES
                r0 = rb * CV_RB + base + j - shift
                src = hbuf_ref if shift == 0 else shifted_ref.at[shift - 1]
                acc = acc + dww_ref[j:j + 1, csl] * src[r0:r0 + CV_RB, csl]
            cbuf_ref[rb * CV_RB:(rb + 1) * CV_RB, csl] = acc
    tail = hbuf_ref[tt:tt + CV_HALO, :]
    hbuf_ref[0:CV_HALO, :] = tail

    hcv = _silu(_layer_norm(cbuf_ref[...] + dwb_ref[...], lng_ref[...], lnb_ref[...])).astype(BF16)
    for c in range(D_MODEL // MERGE_FC):
        sl = slice(c * MERGE_FC, (c + 1) * MERGE_FC)
        y_cv = _dot(hcv, wcv_ref[:, sl]) + bcv_ref[:, sl]
        merged_ref[:, sl] = (pdn_ref[:, sl] + gcv_ref[:, sl] * y_cv + pmla_ref[:, sl]).astype(BF16)
    for r0 in range(0, tt, FFN_RB):
        rows = slice(r0, min(r0 + FFN_RB, tt))
        y = DEEPNORM_ALPHA * x[rows] + _dot(merged_ref[rows, :], wout_ref[...])
        o_ref[rows, :] = _layer_norm(y, g_ref[...], b_ref[...])


def _merge(x, odn, omla, batch, seq, consts):
    n = x.shape[0]
    tt = min(MERGE_TM, seq)
    nt = seq // tt
    row = lambda w: pl.BlockSpec((tt, w), lambda b, t: (b * nt + t, 0))
    wide = pltpu.VMEM((tt, D_MODEL), F32)
    return pl.pallas_call(
        _merge_kernel,
        grid=(batch, nt),
        in_specs=[row(D_MODEL), row(DN_DIM), row(MLA_HEADS * MLA_V)] + [spec for _, spec in consts],
        out_specs=row(D_MODEL),
        out_shape=jax.ShapeDtypeStruct((n, D_MODEL), F32),
        scratch_shapes=[pltpu.VMEM((CV_HALO + tt, CV_CHANNELS), F32),
                        pltpu.VMEM((SUBLANES - 1, CV_HALO + tt - SUBLANES, CV_CHANNELS), F32),
                        pltpu.VMEM((tt, CV_CHANNELS), F32),
                        wide, wide, wide,
                        pltpu.VMEM((tt, D_MODEL), BF16)],
        compiler_params=_params(("parallel", "arbitrary")),
        name="merge",
    )(x, odn, omla, *[a for a, _ in consts])


_IN_SIZES = (DN_DIM, DN_DIM, DN_DIM, DN_DIM, DN_HEADS, DN_HEADS, 2 * CV_CHANNELS,
             MLA_Q_LORA, MLA_KV_LORA, MLA_ROPE, N_BRANCHES * D_MODEL)


_IN_OFFS = tuple(int(v) for v in np.concatenate([[0], np.cumsum(_IN_SIZES)]))
_DN_COLS = _IN_OFFS[4] + LANES


def _pad_lanes(v, width):
    return jnp.pad(v.astype(F32), ((0, 0), (0, width - v.shape[1])))


def _split_w_in(w_in):
    part = lambda i, j=None: w_in[:, :, _IN_OFFS[i]:_IN_OFFS[(i if j is None else j) + 1]]
    pad = jnp.zeros(w_in.shape[:2] + (LANES - MLA_ROPE,), w_in.dtype)
    return dict(
        dn=w_in[:, :, :_DN_COLS].astype(BF16),
        latent=jnp.concatenate([part(7, 9), pad], axis=2).astype(BF16),
        wide=jnp.concatenate([part(6), part(10)], axis=2).astype(BF16),
    )


def _pad_w_uq(w_uq):
    depth = w_uq.shape[0]
    w = w_uq.reshape(depth, MLA_Q_LORA, MLA_HEADS, MLA_NOPE + MLA_ROPE)
    w = jnp.pad(w, ((0, 0), (0, 0), (0, 0), (0, MLA_QK_PAD - MLA_NOPE - MLA_ROPE)))
    return w.reshape(depth, MLA_Q_LORA, MLA_HEADS * MLA_QK_PAD).astype(BF16)


def kernel(x, positions, ln1_g, ln1_b, ffn1_w_gate, ffn1_w_up, ffn1_w_down, w_in, b_gate, dn_conv_w, dn_a_log, dn_dt_bias, dn_norm_w, dn_w_o, cv_glu_b, cv_dw_w, cv_dw_b, cv_ln_g, cv_ln_b, cv_w_pw2, cv_b_pw2, mla_q_norm_w, mla_w_uq, mla_kv_norm_w, mla_w_ukv, mla_w_o, w_out, ln2_g, ln2_b, ffn2_w_gate, ffn2_w_up, ffn2_w_down, ln3_g, ln3_b):
    batch, seq, d = x.shape
    n = batch * seq
    h = x.reshape(n, d)

    half = MLA_ROPE // 2
    inv_freq = ROPE_THETA ** (-np.arange(half, dtype=np.float32) / half)
    freq = np.zeros((1, LANES), np.float32)
    freq[0, :half] = inv_freq
    freq[0, half:MLA_ROPE] = inv_freq
    rope_c, rope_s1, rope_s2 = _rope_tables(positions.astype(F32).reshape(n, 1), jnp.asarray(freq))

    bf = lambda a: a.astype(BF16)
    w = _split_w_in(w_in)
    ffn1 = (bf(ffn1_w_gate), bf(ffn1_w_up), bf(ffn1_w_down))
    ffn2 = (bf(ffn2_w_gate), bf(ffn2_w_up), bf(ffn2_w_down))
    dn_w_o_b, cv_w_pw2_b, mla_w_o_b, w_out_b = bf(dn_w_o), bf(cv_w_pw2), bf(mla_w_o), bf(w_out)
    w_uq_b, w_ukv_b = _pad_w_uq(mla_w_uq), bf(mla_w_ukv)
    a_log_p, dt_bias_p = _pad_lanes(dn_a_log, LANES), _pad_lanes(dn_dt_bias, LANES)
    dn_conv_w, cv_dw_w = dn_conv_w.astype(F32), cv_dw_w.astype(F32)

    for l in range(DEPTH):
        h = _ffn_ln(h, [_layer(a, l) for a in ffn1] + [_vec(ln1_g, l), _vec(ln1_b, l)])
        o_dn = _deltanet(h, batch, seq, [
            _layer(w["dn"], l, 3 * DN_DIM, 0), _layer(w["dn"], l, DN_DIM, 3),
            _layer(w["dn"], l, LANES, _IN_OFFS[4] // LANES), _layer(dn_conv_w, l),
            _vec(a_log_p, l), _vec(dt_bias_p, l), _vec(dn_norm_w, l)])
        q, k, v = _mla_prep(h, [
            _layer(w["latent"], l), _vec(mla_q_norm_w, l), _vec(mla_kv_norm_w, l),
            _layer(w_uq_b, l), _layer(w_ukv_b, l)], rope_c, rope_s1, rope_s2)
        o_mla = _attention(q, k, v, batch, seq)
        h = _merge(h, o_dn, o_mla, batch, seq, [
            _layer(w["wide"], l, 2 * CV_CHANNELS, 0), _vec(cv_glu_b, l), _layer(cv_dw_w, l),
            _vec(cv_dw_b, l), _vec(cv_ln_g, l), _vec(cv_ln_b, l),
            _layer(w["wide"], l, D_MODEL, 1), _layer(w["wide"], l, D_MODEL, 2), _layer(w["wide"], l, D_MODEL, 3),
            _vec(b_gate, l), _layer(dn_w_o_b, l), _layer(cv_w_pw2_b, l), _vec(cv_b_pw2, l),
            _layer(mla_w_o_b, l), _layer(w_out_b, l), _vec(ln2_g, l), _vec(ln2_b, l)])
        h = _ffn_ln(h, [_layer(a, l) for a in ffn2] + [_vec(ln3_g, l), _vec(ln3_b, l)])
    return h.reshape(batch, seq, d)
```
